```python
import math
import jax, jax.numpy as jnp
from jax import lax
import numpy as np

D_MODEL = 2048
BATCH = 8
SEQ = 2048
DEPTH = 2

GRID_W = 64
CTX_LEN = 256
N_MIXERS = 2
HEAD_DIM = 128
N_HEADS = D_MODEL // HEAD_DIM
N_KV_HEADS = N_HEADS // 4
GROUP = N_HEADS // N_KV_HEADS
WINDOW = 128
BLOCK = 128
ROPE_BASE = 10000.0
CONV_WIDTH = 3
D_FF = 4 * D_MODEL
N_MOD = 6
EPS = 1e-6
NEG = -1e30
N_CONV_LAYERS = (DEPTH + 1) // 2
N_ATTN_LAYERS = DEPTH // 2

kernel_name = "hybrid_shortconv_swa_dit_block"


def _rmsnorm(x, g):
    x32 = x.astype(jnp.float32)
    y = x32 * lax.rsqrt(jnp.mean(x32 * x32, axis=-1, keepdims=True) + EPS)
    return y.astype(x.dtype) * g


def _axial_rope_tables(seq_len):
    rows_n = seq_len // GRID_W
    row = jnp.repeat(jnp.arange(rows_n), GRID_W).astype(jnp.float32)
    col = jnp.tile(jnp.arange(GRID_W), rows_n).astype(jnp.float32)
    nf = HEAD_DIM // 4
    inv_freq = ROPE_BASE ** (-jnp.arange(nf, dtype=jnp.float32) / nf)
    ang_r = row[:, None] * inv_freq[None, :]
    ang_c = col[:, None] * inv_freq[None, :]
    ang = jnp.concatenate([ang_r, ang_r, ang_c, ang_c], axis=-1)
    return jnp.cos(ang), jnp.sin(ang)


def _rotate_half_axial(x):
    shp = x.shape
    xr = x.reshape(shp[:-1] + (2, 2, HEAD_DIM // 4))
    rot = jnp.stack([-xr[..., 1, :], xr[..., 0, :]], axis=-2)
    return rot.reshape(shp)


def _apply_rope(x, cos, sin):
    return x * cos.astype(x.dtype) + _rotate_half_axial(x) * sin.astype(x.dtype)


def _short_conv_mixer(u, w_in, w_conv, w_out):
    b_gate, c_gate, hval = jnp.split(u @ w_in, 3, axis=-1)
    z = c_gate * hval
    zp = jnp.pad(z, ((0, 0), (1, 1), (0, 0)))
    conv = zp[:, :-2] * w_conv[0] + zp[:, 1:-1] * w_conv[1] + zp[:, 2:] * w_conv[2]
    return (b_gate * conv) @ w_out


def _split_qkv(t):
    b, n, _ = t.shape
    q, k, v = jnp.split(t, [N_HEADS * HEAD_DIM, (N_HEADS + N_KV_HEADS) * HEAD_DIM], axis=-1)
    return (q.reshape(b, n, N_KV_HEADS, GROUP, HEAD_DIM),
            k.reshape(b, n, N_KV_HEADS, HEAD_DIM),
            v.reshape(b, n, N_KV_HEADS, HEAD_DIM))


def _window_attention(u, uc, w_qkv, sink, w_o, ctx_out):
    b, s, _ = u.shape
    scale = 1.0 / math.sqrt(HEAD_DIM)
    q, k, v = _split_qkv(u @ w_qkv)
    qc, kc, vc = _split_qkv(uc @ w_qkv)
    cos, sin = _axial_rope_tables(s)
    q = _apply_rope(q, cos[:, None, None, :], sin[:, None, None, :])
    k = _apply_rope(k, cos[:, None, :], sin[:, None, :])

    nb = s // BLOCK
    qb = q.reshape(b, nb, BLOCK, N_KV_HEADS, GROUP, HEAD_DIM)
    pad = ((0, 0), (BLOCK, BLOCK), (0, 0), (0, 0))
    kp = jnp.pad(k, pad).reshape(b, nb + 2, BLOCK, N_KV_HEADS, HEAD_DIM)
    vp = jnp.pad(v, pad).reshape(b, nb + 2, BLOCK, N_KV_HEADS, HEAD_DIM)
    kw = jnp.concatenate([kp[:, :-2], kp[:, 1:-1], kp[:, 2:]], axis=2)
    vw = jnp.concatenate([vp[:, :-2], vp[:, 1:-1], vp[:, 2:]], axis=2)

    blk = jnp.arange(nb)[:, None, None] * BLOCK
    qi = blk + jnp.arange(BLOCK)[None, :, None]
    kj = blk - BLOCK + jnp.arange(3 * BLOCK)[None, None, :]
    valid = (jnp.abs(qi - kj) <= WINDOW) & (kj >= 0) & (kj < s)

    s_loc = jnp.einsum('bnqhgd,bnkhd->bnhgqk', qb, kw).astype(jnp.float32) * scale
    s_loc = jnp.where(valid[None, :, None, None], s_loc, NEG)
    s_ctx = jnp.einsum('bnqhgd,bchd->bnhgqc', qb, kc).astype(jnp.float32) * scale
    sink_h = sink.astype(jnp.float32).reshape(N_KV_HEADS, GROUP)
    s_sink = jnp.broadcast_to(sink_h[None, None, :, :, None, None], s_loc.shape[:-1] + (1,))
    p = jax.nn.softmax(jnp.concatenate([s_loc, s_ctx, s_sink], axis=-1), axis=-1)
    n_loc = 3 * BLOCK
    n_ctx = kc.shape[1]
    p_loc = p[..., :n_loc].astype(v.dtype)
    p_ctx = p[..., n_loc:n_loc + n_ctx].astype(v.dtype)
    o = (jnp.einsum('bnhgqk,bnkhd->bnqhgd', p_loc, vw)
         + jnp.einsum('bnhgqc,bchd->bnqhgd', p_ctx, vc))
    y = o.reshape(b, s, N_HEADS * HEAD_DIM) @ w_o

    yc = None
    if ctx_out:
        sc = jnp.einsum('bqhgd,bkhd->bhgqk', qc, kc).astype(jnp.float32) * scale
        sc_sink = jnp.broadcast_to(sink_h[None, :, :, None, None], sc.shape[:-1] + (1,))
        pc = jax.nn.softmax(jnp.concatenate([sc, sc_sink], axis=-1), axis=-1)[..., :n_ctx]
        oc = jnp.einsum('bhgqk,bkhd->bqhgd', pc.astype(vc.dtype), vc)
        yc = oc.reshape(b, n_ctx, N_HEADS * HEAD_DIM) @ w_o
    return y, yc


def _sq_relu_mlp(u, w1, w2):
    return jnp.square(jax.nn.relu(u @ w1)) @ w2


def setup_inputs(seed: int = 0) -> dict:
    key = jax.random.key(seed)
    ks = jax.random.split(key, 20)
    d = D_MODEL
    qkv_w = (N_HEADS + 2 * N_KV_HEADS) * HEAD_DIM

    def nrm(k, shape, scale):
        return jax.random.normal(k, shape, jnp.float32) * scale

    return {
        "x": nrm(ks[0], (BATCH, SEQ, d), 1.0),
        "c": nrm(ks[1], (BATCH, d), 1.0),
        "ctx": nrm(ks[2], (BATCH, CTX_LEN, d), 1.0),
        "c_ctx": nrm(ks[3], (d,), 1.0),
        "norm1_g": 1.0 + nrm(ks[4], (DEPTH, d), 0.02),
        "norm2_g": 1.0 + nrm(ks[5], (DEPTH, d), 0.02),
        "mod_w": nrm(ks[6], (DEPTH, d, N_MOD * d), 0.5 * d ** -0.5),
        "mod_b": nrm(ks[7], (DEPTH, N_MOD * d), 0.02),
        "conv_w_in": nrm(ks[8], (N_CONV_LAYERS, d, 3 * d), d ** -0.5),
        "conv_w": nrm(ks[9], (N_CONV_LAYERS, CONV_WIDTH, d), CONV_WIDTH ** -0.5),
        "conv_w_out": nrm(ks[10], (N_CONV_LAYERS, d, d), d ** -0.5),
        "attn_w_qkv": nrm(ks[11], (N_ATTN_LAYERS, d, qkv_w), d ** -0.5),
        "attn_sink": nrm(ks[12], (N_ATTN_LAYERS, N_HEADS), 0.5),
        "attn_w_o": nrm(ks[13], (N_ATTN_LAYERS, N_HEADS * HEAD_DIM, d), (N_HEADS * HEAD_DIM) ** -0.5),
        "mlp_w1": nrm(ks[14], (DEPTH, d, D_FF), d ** -0.5),
        "mlp_w2": nrm(ks[15], (DEPTH, D_FF, d), D_FF ** -0.5),
        "final_g": 1.0 + nrm(ks[16], (d,), 0.02),
    }


def reference(x, c, ctx, c_ctx, norm1_g, norm2_g, mod_w, mod_b, conv_w_in, conv_w, conv_w_out,
              attn_w_qkv, attn_sink, attn_w_o, mlp_w1, mlp_w2, final_g):
    h = x
    hc = ctx
    sc_lat = jax.nn.silu(c)
    sc_ctx = jax.nn.silu(c_ctx)
    for i in range(DEPTH):
        last = i == DEPTH - 1
        m = sc_lat @ mod_w[i] + mod_b[i]
        mc = sc_ctx @ mod_w[i] + mod_b[i]
        sh1, s1, g1, sh2, s2, g2 = jnp.split(m[:, None, :], N_MOD, axis=-1)
        csh1, cs1, cg1, csh2, cs2, cg2 = jnp.split(mc, N_MOD, axis=-1)

        u = _rmsnorm(h, norm1_g[i]) * (1.0 + s1) + sh1
        if i % N_MIXERS == 0:
            j = i // N_MIXERS
            y = _short_conv_mixer(u, conv_w_in[j], conv_w[j], conv_w_out[j])
            if not last:
                uc = _rmsnorm(hc, norm1_g[i]) * (1.0 + cs1) + csh1
                yc = _short_conv_mixer(uc, conv_w_in[j], conv_w[j], conv_w_out[j])
        else:
            j = i // N_MIXERS
            uc = _rmsnorm(hc, norm1_g[i]) * (1.0 + cs1) + csh1
            y, yc = _window_attention(u, uc, attn_w_qkv[j], attn_sink[j], attn_w_o[j], not last)
        h = h + g1 * y

        u2 = _rmsnorm(h, norm2_g[i]) * (1.0 + s2) + sh2
        h = h + g2 * _sq_relu_mlp(u2, mlp_w1[i], mlp_w2[i])
        if not last:
            hc = hc + cg1 * yc
            uc2 = _rmsnorm(hc, norm2_g[i]) * (1.0 + cs2) + csh2
            hc = hc + cg2 * _sq_relu_mlp(uc2, mlp_w1[i], mlp_w2[i])
    return _rmsnorm(h, final_g)
```

```python
import functools
import math

import jax
import jax.numpy as jnp
from jax import lax
from jax.experimental import pallas as pl
from jax.experimental.pallas import tpu as pltpu

F32 = jnp.float32
BF16 = jnp.bfloat16

D_MODEL = 2048
BATCH = 8
SEQ = 2048
CTX_LEN = 256
GRID_W = 64
HEAD_DIM = 128
N_HEADS = 16
N_KV_HEADS = 4
GROUP = 4
WINDOW = 128
ROPE_BASE = 10000.0
D_FF = 4 * D_MODEL
N_MOD = 6
EPS = 1e-6
NEG = -1e30

T_LAT = BATCH * SEQ
T_CTX = BATCH * CTX_LEN
T_ALL = T_LAT + T_CTX
MOD_ROWS = 16
QKV_W = (N_HEADS + 2 * N_KV_HEADS) * HEAD_DIM
KWIN = 3 * WINDOW

VMEM_LIMIT = 56 * 1024 * 1024


def _params(sem):
    return pltpu.CompilerParams(dimension_semantics=sem, vmem_limit_bytes=VMEM_LIMIT)


def _mod_row(i, tm):
    return jnp.minimum((i * tm) // SEQ, BATCH)


def _mod_kernel(c_ref, w_ref, b_ref, o_ref):
    s = jax.nn.silu(c_ref[...]).astype(BF16)
    o_ref[...] = jnp.dot(s, w_ref[...].astype(BF16), preferred_element_type=F32) + b_ref[...]


def _mod_table(cvec, mod_w, mod_b):
    depth = mod_w.shape[0]
    n = N_MOD * D_MODEL
    tn = 1024
    out = pl.pallas_call(
        _mod_kernel,
        grid=(depth, n // tn),
        in_specs=[
            pl.BlockSpec((MOD_ROWS, D_MODEL), lambda l, j: (0, 0)),
            pl.BlockSpec((None, D_MODEL, tn), lambda l, j: (l, 0, j)),
            pl.BlockSpec((None, 1, tn), lambda l, j: (l, 0, j)),
        ],
        out_specs=pl.BlockSpec((None, MOD_ROWS, tn), lambda l, j: (l, 0, j)),
        out_shape=jax.ShapeDtypeStruct((depth, MOD_ROWS, n), F32),
        compiler_params=_params(("parallel", "parallel")),
        name="mod_table",
    )(cvec, mod_w, mod_b.reshape(depth, 1, n))
    return out.reshape(depth, MOD_ROWS, 1, n)


def _norm_mod_kernel(x_ref, g_ref, sc_ref, sh_ref, o_ref):
    x = x_ref[...]
    y = x * lax.rsqrt(jnp.mean(x * x, axis=-1, keepdims=True) + EPS)
    y = y * g_ref[...]
    o_ref[...] = (y * (1.0 + sc_ref[...]) + sh_ref[...]).astype(o_ref.dtype)


def _norm_mod(h, g, mod, shift_blk, scale_blk, rows):
    tm = 512
    return pl.pallas_call(
        _norm_mod_kernel,
        grid=(rows // tm,),
        in_specs=[
            pl.BlockSpec((tm, D_MODEL), lambda i: (i, 0)),
            pl.BlockSpec((1, D_MODEL), lambda i: (0, 0)),
            pl.BlockSpec((None, 1, D_MODEL), lambda i: (_mod_row(i, tm), 0, scale_blk)),
            pl.BlockSpec((None, 1, D_MODEL), lambda i: (_mod_row(i, tm), 0, shift_blk)),
        ],
        out_specs=pl.BlockSpec((tm, D_MODEL), lambda i: (i, 0)),
        out_shape=jax.ShapeDtypeStruct((rows, D_MODEL), BF16),
        compiler_params=_params(("parallel",)),
        name="norm_mod",
    )(h, g.reshape(1, D_MODEL), mod, mod)


def _final_norm_kernel(x_ref, g_ref, o_ref):
    x = x_ref[...]
    y = x * lax.rsqrt(jnp.mean(x * x, axis=-1, keepdims=True) + EPS)
    o_ref[...] = y * g_ref[...]


def _final_norm(h, g, rows):
    tm = 512
    return pl.pallas_call(
        _final_norm_kernel,
        grid=(rows // tm,),
        in_specs=[
            pl.BlockSpec((tm, D_MODEL), lambda i: (i, 0)),
            pl.BlockSpec((1, D_MODEL), lambda i: (0, 0)),
        ],
        out_specs=pl.BlockSpec((tm, D_MODEL), lambda i: (i, 0)),
        out_shape=jax.ShapeDtypeStruct((rows, D_MODEL), F32),
        compiler_params=_params(("parallel",)),
        name="final_norm",
    )(h, g.reshape(1, D_MODEL))


def _mm_kernel(*refs, nw, ne, no, nk, epilogue):
    x_ref = refs[0]
    w_refs = refs[1:1 + nw]
    e_refs = refs[1 + nw:1 + nw + ne]
    o_refs = refs[1 + nw + ne:1 + nw + ne + no]
    acc_refs = refs[1 + nw + ne + no:]

    def finish(accs):
        for o_ref, val in zip(o_refs, epilogue(accs, e_refs)):
            o_ref[...] = val.astype(o_ref.dtype)

    if nk == 1:
        x = x_ref[...]
        finish([jnp.dot(x, w[...], preferred_element_type=F32) for w in w_refs])
        return

    k = pl.program_id(2)

    @pl.when(k == 0)
    def _():
        for acc, w in zip(acc_refs, w_refs):
            acc[...] = jnp.dot(x_ref[...], w[...], preferred_element_type=F32)

    @pl.when(k > 0)
    def _():
        for acc, w in zip(acc_refs, w_refs):
            acc[...] += jnp.dot(x_ref[...], w[...], preferred_element_type=F32)

    @pl.when(k == nk - 1)
    def _():
        finish([acc[...] for acc in acc_refs])


def _matmul(x, ws, extras, outs, epilogue, *, rows, tm, tn, tk, n_cols, row_blk_off=0, name):
    kdim = x.shape[1]
    nk = kdim // tk
    grid = (rows // tm, n_cols // tn, nk)
    in_specs = [pl.BlockSpec((tm, tk), lambda i, j, k: (i + row_blk_off, k))]
    args = [x]
    for w, off in ws:
        in_specs.append(pl.BlockSpec((tk, tn), lambda i, j, k, off=off: (k, j + off)))
        args.append(w)
    for arr, blk, imap in extras:
        in_specs.append(pl.BlockSpec(blk, lambda i, j, k, imap=imap: imap(i, j)))
        args.append(arr)
    out_specs = [pl.BlockSpec((tm, tn), lambda i, j, k: (i, j)) for _ in outs]
    out_shape = [jax.ShapeDtypeStruct((rows, cols), dt) for cols, dt in outs]
    scratch = [pltpu.VMEM((tm, tn), F32) for _ in ws] if nk > 1 else []
    body = functools.partial(_mm_kernel, nw=len(ws), ne=len(extras), no=len(outs), nk=nk,
                             epilogue=epilogue)
    res = pl.pallas_call(
        body,
        grid=grid,
        in_specs=in_specs,
        out_specs=out_specs,
        out_shape=out_shape,
        scratch_shapes=scratch,
        compiler_params=_params(("parallel", "parallel", "arbitrary")),
        name=name,
    )(*args)
    return res


def _epi_gate(accs, e_refs):
    b_gate, c_gate, hval = accs
    return [b_gate, c_gate * hval]


def _epi_relu2(accs, e_refs):
    a = jnp.maximum(accs[0], 0.0)
    return [a * a]


def _epi_plain(accs, e_refs):
    return [accs[0]]


def _epi_residual(accs, e_refs):
    h_ref, gate_ref = e_refs
    return [h_ref[...] + gate_ref[...] * accs[0]]


def _epi_rope(accs, e_refs):
    cos_ref, sin_lo_ref, sin_hi_ref = e_refs
    acc = accs[0]
    cos, sin_lo, sin_hi = cos_ref[...], sin_lo_ref[...], sin_hi_ref[...]
    heads = []
    for hh in range(acc.shape[1] // HEAD_DIM):
        a = acc[:, hh * HEAD_DIM:(hh + 1) * HEAD_DIM]
        up = pltpu.roll(a, HEAD_DIM - HEAD_DIM // 4, 1)
        dn = pltpu.roll(a, HEAD_DIM // 4, 1)
        heads.append(a * cos + up * sin_lo + dn * sin_hi)
    return [jnp.concatenate(heads, axis=1)]


def _residual_extras(h, mod, gate_blk, tm, tn):
    per = D_MODEL // tn
    return [
        (h, (tm, tn), lambda i, j: (i, j)),
        (mod, (None, 1, tn), lambda i, j: (_mod_row(i, tm), 0, gate_blk * per + j)),
    ]


HALO = 16


def _conv_gate_kernel(b_ref, z_ref, zp_ref, zn_ref, cw_ref, o_ref, pad_ref, *, tm):
    i = pl.program_id(0)
    row0 = i * tm
    seq_len = jnp.where(row0 >= T_LAT, CTX_LEN, SEQ)
    pos = (row0 + lax.broadcasted_iota(jnp.int32, (tm, 1), 0)) & (seq_len - 1)
    pad_ref[pl.ds(8, tm), :] = z_ref[...].astype(F32)
    pad_ref[pl.ds(7, 1), :] = zp_ref[pl.ds(HALO - 1, 1), :].astype(F32)
    pad_ref[pl.ds(8 + tm, 1), :] = zn_ref[pl.ds(0, 1), :].astype(F32)
    z_prev = jnp.where(pos == 0, 0.0, pad_ref[pl.ds(7, tm), :])
    z_next = jnp.where(pos == seq_len - 1, 0.0, pad_ref[pl.ds(9, tm), :])
    z_mid = pad_ref[pl.ds(8, tm), :]
    conv = z_prev * cw_ref[pl.ds(0, 1), :] + z_mid * cw_ref[pl.ds(1, 1), :] + z_next * cw_ref[pl.ds(2, 1), :]
    o_ref[...] = (b_ref[...].astype(F32) * conv).astype(o_ref.dtype)


def _conv_gate(b_gate, z, conv_w, rows):
    tm = 256
    per = tm // HALO
    last = rows // HALO - 1
    z3 = z.reshape(rows // HALO, HALO, D_MODEL)
    return pl.pallas_call(
        functools.partial(_conv_gate_kernel, tm=tm),
        grid=(rows // tm,),
        in_specs=[
            pl.BlockSpec((tm, D_MODEL), lambda i: (i, 0)),
            pl.BlockSpec((tm, D_MODEL), lambda i: (i, 0)),
            pl.BlockSpec((None, HALO, D_MODEL), lambda i: (jnp.maximum(i * per - 1, 0), 0, 0)),
            pl.BlockSpec((None, HALO, D_MODEL), lambda i: (jnp.minimum((i + 1) * per, last), 0, 0)),
            pl.BlockSpec((3, D_MODEL), lambda i: (0, 0)),
        ],
        out_specs=pl.BlockSpec((tm, D_MODEL), lambda i: (i, 0)),
        out_shape=jax.ShapeDtypeStruct((rows, D_MODEL), BF16),
        scratch_shapes=[pltpu.VMEM((tm + 16, D_MODEL), F32)],
        compiler_params=_params(("parallel",)),
        name="conv_gate",
    )(b_gate, z, z3, z3, conv_w)


def _attn_kernel(sink_ref, q_ref, k_ref, v_ref, kc_ref, vc_ref, o_ref):
    kvh = pl.program_id(1)
    kc = kc_ref[...]
    vc = vc_ref[...]
    rel = (lax.broadcasted_iota(jnp.int32, (WINDOW, KWIN), 0)
           - lax.broadcasted_iota(jnp.int32, (WINDOW, KWIN), 1))
    nt = (((1,), (1,)), ((), ()))

    def body(n, carry):
        q0 = pl.multiple_of(n * WINDOW, WINDOW)
        start = pl.multiple_of(jnp.clip((n - 1) * WINDOW, 0, SEQ - KWIN), WINDOW)
        valid = jnp.abs(rel + (q0 - start)) <= WINDOW
        qs = jnp.concatenate(
            [q_ref[pl.ds(q0, WINDOW), g * HEAD_DIM:(g + 1) * HEAD_DIM] for g in range(GROUP)], axis=0)
        kw = k_ref[pl.ds(start, KWIN), :]
        vw = v_ref[pl.ds(start, KWIN), :]
        s_loc = lax.dot_general(qs, kw, nt, preferred_element_type=F32)
        s_ctx = lax.dot_general(qs, kc, nt, preferred_element_type=F32)
        p_loc, p_ctx, dens = [], [], []
        for g in range(GROUP):
            sl = jnp.where(valid, s_loc[g * WINDOW:(g + 1) * WINDOW], NEG)
            sc = s_ctx[g * WINDOW:(g + 1) * WINDOW]
            sk = sink_ref[kvh * GROUP + g]
            m = jnp.maximum(jnp.maximum(jnp.max(sl, axis=-1, keepdims=True),
                                        jnp.max(sc, axis=-1, keepdims=True)), sk)
            el = jnp.exp(sl - m)
            ec = jnp.exp(sc - m)
            dens.append(jnp.sum(el, axis=-1, keepdims=True) + jnp.sum(ec, axis=-1, keepdims=True)
                        + jnp.exp(sk - m))
            p_loc.append(el.astype(BF16))
            p_ctx.append(ec.astype(BF16))
        o = (jnp.dot(jnp.concatenate(p_loc, axis=0), vw, preferred_element_type=F32)
             + jnp.dot(jnp.concatenate(p_ctx, axis=0), vc, preferred_element_type=F32))
        o = o / jnp.concatenate(dens, axis=0)
        for g in range(GROUP):
            o_ref[pl.ds(q0, WINDOW), g * HEAD_DIM:(g + 1) * HEAD_DIM] = (
                o[g * WINDOW:(g + 1) * WINDOW].astype(o_ref.dtype))
        return carry

    lax.fori_loop(0, SEQ // WINDOW, body, 0)


def _attention(qkv, kvc, sink):
    qw = GROUP * HEAD_DIM
    k_off = N_HEADS
    v_off = N_HEADS + N_KV_HEADS
    return pl.pallas_call(
        _attn_kernel,
        grid=(BATCH, N_KV_HEADS),
        in_specs=[
            pl.BlockSpec(memory_space=pltpu.SMEM),
            pl.BlockSpec((SEQ, qw), lambda b, h: (b, h)),
            pl.BlockSpec((SEQ, HEAD_DIM), lambda b, h: (b, k_off + h)),
            pl.BlockSpec((SEQ, HEAD_DIM), lambda b, h: (b, v_off + h)),
            pl.BlockSpec((CTX_LEN, HEAD_DIM), lambda b, h: (b, h)),
            pl.BlockSpec((CTX_LEN, HEAD_DIM), lambda b, h: (b, N_KV_HEADS + h)),
        ],
        out_specs=pl.BlockSpec((SEQ, qw), lambda b, h: (b, h)),
        out_shape=jax.ShapeDtypeStruct((T_LAT, N_HEADS * HEAD_DIM), BF16),
        compiler_params=_params(("parallel", "parallel")),
        name="window_attention",
    )(sink, qkv, qkv, qkv, kvc, kvc)


def _rope_tables():
    rows_n = SEQ // GRID_W
    row = jnp.repeat(jnp.arange(rows_n), GRID_W).astype(F32)
    col = jnp.tile(jnp.arange(GRID_W), rows_n).astype(F32)
    nf = HEAD_DIM // 4
    inv_freq = ROPE_BASE ** (-jnp.arange(nf, dtype=F32) / nf)
    ang_r = row[:, None] * inv_freq[None, :]
    ang_c = col[:, None] * inv_freq[None, :]
    ang = jnp.concatenate([ang_r, ang_r, ang_c, ang_c], axis=-1)
    cos, sin = jnp.cos(ang), jnp.sin(ang)
    low_half = (jnp.arange(HEAD_DIM) % (2 * nf)) < nf
    sin_lo = jnp.where(low_half[None, :], -sin, 0.0)
    sin_hi = jnp.where(low_half[None, :], 0.0, sin)
    scale = 1.0 / math.sqrt(HEAD_DIM)
    ones, zeros = jnp.ones_like(cos), jnp.zeros_like(cos)
    cos3 = jnp.stack([cos * scale, cos, ones])
    lo3 = jnp.stack([sin_lo * scale, sin_lo, zeros])
    hi3 = jnp.stack([sin_hi * scale, sin_hi, zeros])
    return cos3, lo3, hi3


def _mlp(h, u2, mod, w1, w2, rows):
    a = _matmul(u2, [(w1, 0)], [], [(D_FF, BF16)], _epi_relu2,
                rows=rows, tm=1024, tn=1024, tk=D_MODEL, n_cols=D_FF, name="mlp_up")[0]
    tm, tn = 1024, 1024
    return _matmul(a, [(w2, 0)], _residual_extras(h, mod, 5, tm, tn), [(D_MODEL, F32)], _epi_residual,
                   rows=rows, tm=tm, tn=tn, tk=2048, n_cols=D_MODEL, name="mlp_down")[0]


def _out_proj(h, v, mod, w_out, rows):
    tm, tn = 1024, 1024
    return _matmul(v, [(w_out, 0)], _residual_extras(h, mod, 2, tm, tn), [(D_MODEL, F32)], _epi_residual,
                   rows=rows, tm=tm, tn=tn, tk=D_MODEL, n_cols=D_MODEL, name="out_proj")[0]


def kernel(x, c, ctx, c_ctx, norm1_g, norm2_g, mod_w, mod_b, conv_w_in, conv_w, conv_w_out,
           attn_w_qkv, attn_sink, attn_w_o, mlp_w1, mlp_w2, final_g):
    h = jnp.concatenate([x.reshape(T_LAT, D_MODEL), ctx.reshape(T_CTX, D_MODEL)], axis=0)
    cvec = jnp.concatenate(
        [c, c_ctx[None, :], jnp.zeros((MOD_ROWS - BATCH - 1, D_MODEL), F32)], axis=0)
    mod = _mod_table(cvec, mod_w, mod_b)

    w_in = conv_w_in[0].astype(BF16)
    w_out = conv_w_out[0].astype(BF16)
    w_qkv = attn_w_qkv[0].astype(BF16)
    w_o = attn_w_o[0].astype(BF16)
    w1 = mlp_w1.astype(BF16)
    w2 = mlp_w2.astype(BF16)

    m0 = mod[0]
    u = _norm_mod(h, norm1_g[0], m0, 0, 1, T_ALL)
    nblk = D_MODEL // 512
    b_gate, z = _matmul(u, [(w_in, 0), (w_in, nblk), (w_in, 2 * nblk)], [],
                        [(D_MODEL, BF16), (D_MODEL, BF16)], _epi_gate,
                        rows=T_ALL, tm=1024, tn=512, tk=D_MODEL, n_cols=D_MODEL, name="conv_in_proj")
    v = _conv_gate(b_gate, z, conv_w[0], T_ALL)
    h = _out_proj(h, v, m0, w_out, T_ALL)
    u2 = _norm_mod(h, norm2_g[0], m0, 3, 4, T_ALL)
    h = _mlp(h, u2, m0, w1[0], w2[0], T_ALL)

    m1 = mod[1]
    u = _norm_mod(h, norm1_g[1], m1, 0, 1, T_ALL)
    cos3, lo3, hi3 = _rope_tables()
    tm, tn = 1024, 512
    pos_blocks = SEQ // tm
    qcols = N_HEADS * HEAD_DIM // tn

    def table_map(i, j):
        return (jnp.clip(j - (qcols - 1), 0, 2), i % pos_blocks, 0)

    rope_extras = [(t, (None, tm, HEAD_DIM), table_map) for t in (cos3, lo3, hi3)]
    qkv = _matmul(u, [(w_qkv, 0)], rope_extras, [(QKV_W, BF16)], _epi_rope,
                  rows=T_LAT, tm=tm, tn=tn, tk=D_MODEL, n_cols=QKV_W, name="qkv_latent")[0]
    kv_cols = 2 * N_KV_HEADS * HEAD_DIM
    kvc = _matmul(u, [(w_qkv, N_HEADS * HEAD_DIM // 512)], [], [(kv_cols, BF16)], _epi_plain,
                  rows=T_CTX, tm=1024, tn=512, tk=D_MODEL, n_cols=kv_cols,
                  row_blk_off=T_LAT // 1024, name="kv_context")[0]
    o = _attention(qkv, kvc, attn_sink[0])
    h = _out_proj(h, o, m1, w_o, T_LAT)
    u2 = _norm_mod(h, norm2_g[1], m1, 3, 4, T_LAT)
    h = _mlp(h, u2, m1, w1[1], w2[1], T_LAT)
    out = _final_norm(h, final_g, T_LAT)
    return out.reshape(BATCH, SEQ, D_MODEL)
```

```python
import functools
import math

import jax
import jax.numpy as jnp
from jax import lax
from jax.experimental import pallas as pl
from jax.experimental.pallas import tpu as pltpu

F32 = jnp.float32
BF16 = jnp.bfloat16

D_MODEL = 2048
BATCH = 8
SEQ = 2048
CTX_LEN = 256
GRID_W = 64
HEAD_DIM = 128
N_HEADS = 16
N_KV_HEADS = 4
GROUP = 4
WINDOW = 128
ROPE_BASE = 10000.0
D_FF = 4 * D_MODEL
N_MOD = 6
EPS = 1e-6
NEG = -1e30
LOG2E = math.log2(math.e)

T_LAT = BATCH * SEQ
T_CTX = BATCH * CTX_LEN
T_ALL = T_LAT + T_CTX
MOD_ROWS = 16
QKV_W = (N_HEADS + 2 * N_KV_HEADS) * HEAD_DIM
KWIN = 3 * WINDOW

VMEM_LIMIT = 56 * 1024 * 1024


def _params(sem):
    return pltpu.CompilerParams(dimension_semantics=sem, vmem_limit_bytes=VMEM_LIMIT)


def _mod_row(i, tm):
    return jnp.minimum((i * tm) // SEQ, BATCH)


def _mod_kernel(c_ref, w_ref, b_ref, o_ref):
    s = jax.nn.silu(c_ref[...]).astype(BF16)
    o_ref[...] = jnp.dot(s, w_ref[...].astype(BF16), preferred_element_type=F32) + b_ref[...]


def _mod_table(cvec, mod_w, mod_b):
    depth = mod_w.shape[0]
    n = N_MOD * D_MODEL
    tn = 1024
    out = pl.pallas_call(
        _mod_kernel,
        grid=(depth, n // tn),
        in_specs=[
            pl.BlockSpec((MOD_ROWS, D_MODEL), lambda l, j: (0, 0)),
            pl.BlockSpec((None, D_MODEL, tn), lambda l, j: (l, 0, j)),
            pl.BlockSpec((None, 1, tn), lambda l, j: (l, 0, j)),
        ],
        out_specs=pl.BlockSpec((None, MOD_ROWS, tn), lambda l, j: (l, 0, j)),
        out_shape=jax.ShapeDtypeStruct((depth, MOD_ROWS, n), F32),
        compiler_params=_params(("parallel", "parallel")),
        name="mod_table",
    )(cvec, mod_w, mod_b.reshape(depth, 1, n))
    return out.reshape(depth, MOD_ROWS, 1, n)


def _norm_mod_kernel(x_ref, g_ref, sc_ref, sh_ref, o_ref):
    x = x_ref[...]
    y = x * lax.rsqrt(jnp.mean(x * x, axis=-1, keepdims=True) + EPS)
    y = y * g_ref[...]
    o_ref[...] = (y * (1.0 + sc_ref[...]) + sh_ref[...]).astype(o_ref.dtype)


def _norm_mod(h, g, mod, shift_blk, scale_blk, rows):
    tm = 512
    return pl.pallas_call(
        _norm_mod_kernel,
        grid=(rows // tm,),
        in_specs=[
            pl.BlockSpec((tm, D_MODEL), lambda i: (i, 0)),
            pl.BlockSpec((1, D_MODEL), lambda i: (0, 0)),
            pl.BlockSpec((None, 1, D_MODEL), lambda i: (_mod_row(i, tm), 0, scale_blk)),
            pl.BlockSpec((None, 1, D_MODEL), lambda i: (_mod_row(i, tm), 0, shift_blk)),
        ],
        out_specs=pl.BlockSpec((tm, D_MODEL), lambda i: (i, 0)),
        out_shape=jax.ShapeDtypeStruct((rows, D_MODEL), BF16),
        compiler_params=_params(("parallel",)),
        name="norm_mod",
    )(h, g.reshape(1, D_MODEL), mod, mod)


def _final_norm_kernel(x_ref, g_ref, o_ref):
    x = x_ref[...]
    y = x * lax.rsqrt(jnp.mean(x * x, axis=-1, keepdims=True) + EPS)
    o_ref[...] = y * g_ref[...]


def _final_norm(h, g, rows):
    tm = 512
    return pl.pallas_call(
        _final_norm_kernel,
        grid=(rows // tm,),
        in_specs=[
            pl.BlockSpec((tm, D_MODEL), lambda i: (i, 0)),
            pl.BlockSpec((1, D_MODEL), lambda i: (0, 0)),
        ],
        out_specs=pl.BlockSpec((tm, D_MODEL), lambda i: (i, 0)),
        out_shape=jax.ShapeDtypeStruct((rows, D_MODEL), F32),
        compiler_params=_params(("parallel",)),
        name="final_norm",
    )(h, g.reshape(1, D_MODEL))


def _ws_kernel(*refs, nw, ne, no, row_chunks, epilogue):
    x_ref = refs[0]
    w_refs = refs[1:1 + nw]
    e_refs = refs[1 + nw:1 + nw + ne]
    o_refs = refs[1 + nw + ne:1 + nw + ne + no]
    strip_refs = refs[1 + nw + ne + no:]

    @pl.when(pl.program_id(1) == 0)
    def _():
        for w_ref, strip in zip(w_refs, strip_refs):
            strip[...] = w_ref[...].astype(BF16)

    rc = x_ref.shape[0] // row_chunks
    for r in range(row_chunks):
        rows = pl.ds(r * rc, rc)
        x = x_ref[rows, :]
        accs = [jnp.dot(x, strip[...], preferred_element_type=F32) for strip in strip_refs]
        for o_ref, val in zip(o_refs, epilogue(accs, e_refs, rows)):
            o_ref[rows, :] = val.astype(o_ref.dtype)


def _matmul_ws(x, ws, extras, outs, epilogue, *, rows, tm, tn, n_cols, row_blk_off=0, row_chunks=1, name):
    kdim = x.shape[1]
    in_specs = [pl.BlockSpec((tm, kdim), lambda j, i: (i + row_blk_off, 0))]
    args = [x]
    for w, off in ws:
        in_specs.append(pl.BlockSpec((kdim, tn), lambda j, i, off=off: (0, j + off)))
        args.append(w)
    for arr, blk, imap in extras:
        in_specs.append(pl.BlockSpec(blk, lambda j, i, imap=imap: imap(i, j)))
        args.append(arr)
    body = functools.partial(_ws_kernel, nw=len(ws), ne=len(extras), no=len(outs),
                             row_chunks=row_chunks, epilogue=epilogue)
    return pl.pallas_call(
        body,
        grid=(n_cols // tn, rows // tm),
        in_specs=in_specs,
        out_specs=[pl.BlockSpec((tm, tn), lambda j, i: (i, j)) for _ in outs],
        out_shape=[jax.ShapeDtypeStruct((rows, cols), dt) for cols, dt in outs],
        scratch_shapes=[pltpu.VMEM((kdim, tn), BF16) for _ in ws],
        compiler_params=_params(("arbitrary", "arbitrary")),
        name=name,
    )(*args)


def _ks_kernel(x_ref, w_ref, h_ref, gate_ref, o_ref, acc_ref, *, nk):
    k = pl.program_id(2)

    @pl.when(k == 0)
    def _():
        acc_ref[...] = jnp.dot(x_ref[...], w_ref[...], preferred_element_type=F32)

    @pl.when(k > 0)
    def _():
        acc_ref[...] += jnp.dot(x_ref[...], w_ref[...], preferred_element_type=F32)

    @pl.when(k == nk - 1)
    def _():
        o_ref[...] = h_ref[...] + gate_ref[...] * acc_ref[...]


def _matmul_ks_residual(x, w, h, mod, gate_blk, *, rows, tm, tn, tk, name):
    kdim, n_cols = w.shape
    nk = kdim // tk
    per = D_MODEL // tn
    return pl.pallas_call(
        functools.partial(_ks_kernel, nk=nk),
        grid=(rows // tm, n_cols // tn, nk),
        in_specs=[
            pl.BlockSpec((tm, tk), lambda i, j, k: (i, k)),
            pl.BlockSpec((tk, tn), lambda i, j, k: (k, j)),
            pl.BlockSpec((tm, tn), lambda i, j, k: (i, j)),
            pl.BlockSpec((None, 1, tn), lambda i, j, k: (_mod_row(i, tm), 0, gate_blk * per + j)),
        ],
        out_specs=pl.BlockSpec((tm, tn), lambda i, j, k: (i, j)),
        out_shape=jax.ShapeDtypeStruct((rows, n_cols), F32),
        scratch_shapes=[pltpu.VMEM((tm, tn), F32)],
        compiler_params=_params(("parallel", "parallel", "arbitrary")),
        name=name,
    )(x, w, h, mod)


def _epi_gate(accs, e_refs, rows):
    b_gate, c_gate, hval = accs
    return [b_gate, c_gate * hval]


def _epi_relu2(accs, e_refs, rows):
    a = jnp.maximum(accs[0], 0.0)
    return [a * a]


def _epi_plain(accs, e_refs, rows):
    return [accs[0]]


def _epi_residual(accs, e_refs, rows):
    h_ref, gate_ref = e_refs
    return [h_ref[rows, :] + gate_ref[...] * accs[0]]


def _epi_rope(accs, e_refs, rows):
    cos_ref, sin_lo_ref, sin_hi_ref = e_refs
    acc = accs[0]
    cos, sin_lo, sin_hi = cos_ref[rows, :], sin_lo_ref[rows, :], sin_hi_ref[rows, :]
    heads = []
    for hh in range(acc.shape[1] // HEAD_DIM):
        a = acc[:, hh * HEAD_DIM:(hh + 1) * HEAD_DIM]
        up = pltpu.roll(a, HEAD_DIM - HEAD_DIM // 4, 1)
        dn = pltpu.roll(a, HEAD_DIM // 4, 1)
        heads.append(a * cos + up * sin_lo + dn * sin_hi)
    return [jnp.concatenate(heads, axis=1)]


def _residual_extras(h, mod, gate_blk, tm, tn):
    per = D_MODEL // tn
    return [
        (h, (tm, tn), lambda i, j: (i, j)),
        (mod, (None, 1, tn), lambda i, j: (_mod_row(i, tm), 0, gate_blk * per + j)),
    ]


HALO = 16


def _conv_gate_kernel(b_ref, z_ref, zp_ref, zn_ref, cw_ref, o_ref, pad_ref, *, tm):
    i = pl.program_id(0)
    row0 = i * tm
    seq_len = jnp.where(row0 >= T_LAT, CTX_LEN, SEQ)
    pos = (row0 + lax.broadcasted_iota(jnp.int32, (tm, 1), 0)) & (seq_len - 1)
    pad_ref[pl.ds(8, tm), :] = z_ref[...].astype(F32)
    pad_ref[pl.ds(7, 1), :] = zp_ref[pl.ds(HALO - 1, 1), :].astype(F32)
    pad_ref[pl.ds(8 + tm, 1), :] = zn_ref[pl.ds(0, 1), :].astype(F32)
    z_prev = jnp.where(pos == 0, 0.0, pad_ref[pl.ds(7, tm), :])
    z_next = jnp.where(pos == seq_len - 1, 0.0, pad_ref[pl.ds(9, tm), :])
    z_mid = pad_ref[pl.ds(8, tm), :]
    conv = z_prev * cw_ref[pl.ds(0, 1), :] + z_mid * cw_ref[pl.ds(1, 1), :] + z_next * cw_ref[pl.ds(2, 1), :]
    o_ref[...] = (b_ref[...].astype(F32) * conv).astype(o_ref.dtype)


def _conv_gate(b_gate, z, conv_w, rows):
    tm = 256
    per = tm // HALO
    last = rows // HALO - 1
    z3 = z.reshape(rows // HALO, HALO, D_MODEL)
    return pl.pallas_call(
        functools.partial(_conv_gate_kernel, tm=tm),
        grid=(rows // tm,),
        in_specs=[
            pl.BlockSpec((tm, D_MODEL), lambda i: (i, 0)),
            pl.BlockSpec((tm, D_MODEL), lambda i: (i, 0)),
            pl.BlockSpec((None, HALO, D_MODEL), lambda i: (jnp.maximum(i * per - 1, 0), 0, 0)),
            pl.BlockSpec((None, HALO, D_MODEL), lambda i: (jnp.minimum((i + 1) * per, last), 0, 0)),
            pl.BlockSpec((3, D_MODEL), lambda i: (0, 0)),
        ],
        out_specs=pl.BlockSpec((tm, D_MODEL), lambda i: (i, 0)),
        out_shape=jax.ShapeDtypeStruct((rows, D_MODEL), BF16),
        scratch_shapes=[pltpu.VMEM((tm + 16, D_MODEL), F32)],
        compiler_params=_params(("parallel",)),
        name="conv_gate",
    )(b_gate, z, z3, z3, conv_w)


def _attn_kernel(sink_ref, q_ref, k_ref, v_ref, kc_ref, vc_ref, o_ref, va_ref, vca_ref):
    kvh = pl.program_id(1)
    va_ref[:, :HEAD_DIM] = v_ref[...]
    va_ref[:, HEAD_DIM:] = jnp.ones((SEQ, HEAD_DIM), BF16)
    vca_ref[:, :HEAD_DIM] = vc_ref[...]
    vca_ref[:, HEAD_DIM:] = jnp.ones((CTX_LEN, HEAD_DIM), BF16)
    kc = kc_ref[...]
    vca = vca_ref[...]
    rel = (lax.broadcasted_iota(jnp.int32, (WINDOW, KWIN), 0)
           - lax.broadcasted_iota(jnp.int32, (WINDOW, KWIN), 1))
    nt = (((1,), (1,)), ((), ()))

    def body(n, carry):
        q0 = pl.multiple_of(n * WINDOW, WINDOW)
        start = pl.multiple_of(jnp.clip((n - 1) * WINDOW, 0, SEQ - KWIN), WINDOW)
        valid = jnp.abs(rel + (q0 - start)) <= WINDOW
        qs = jnp.concatenate(
            [q_ref[pl.ds(q0, WINDOW), g * HEAD_DIM:(g + 1) * HEAD_DIM] for g in range(GROUP)], axis=0)
        kw = k_ref[pl.ds(start, KWIN), :]
        vwa = va_ref[pl.ds(start, KWIN), :]
        s_loc = lax.dot_general(qs, kw, nt, preferred_element_type=F32)
        s_ctx = lax.dot_general(qs, kc, nt, preferred_element_type=F32)
        p_loc, p_ctx, sink_terms = [], [], []
        for g in range(GROUP):
            sl = jnp.where(valid, s_loc[g * WINDOW:(g + 1) * WINDOW], NEG)
            sc = s_ctx[g * WINDOW:(g + 1) * WINDOW]
            sk = sink_ref[kvh * GROUP + g] * LOG2E
            m = jnp.maximum(jnp.maximum(jnp.max(sl, axis=-1, keepdims=True),
                                        jnp.max(sc, axis=-1, keepdims=True)), sk)
            p_loc.append(jnp.exp2(sl - m).astype(BF16))
            p_ctx.append(jnp.exp2(sc - m).astype(BF16))
            sink_terms.append(jnp.exp2(sk - m))
        oa = (jnp.dot(jnp.concatenate(p_loc, axis=0), vwa, preferred_element_type=F32)
              + jnp.dot(jnp.concatenate(p_ctx, axis=0), vca, preferred_element_type=F32))
        o = oa[:, :HEAD_DIM] / (oa[:, HEAD_DIM:] + jnp.concatenate(sink_terms, axis=0))
        for g in range(GROUP):
            o_ref[pl.ds(q0, WINDOW), g * HEAD_DIM:(g + 1) * HEAD_DIM] = (
                o[g * WINDOW:(g + 1) * WINDOW].astype(o_ref.dtype))
        return carry

    lax.fori_loop(0, SEQ // WINDOW, body, 0, unroll=8)


def _attention(qkv, kvc, sink):
    qw = GROUP * HEAD_DIM
    k_off = N_HEADS
    v_off = N_HEADS + N_KV_HEADS
    return pl.pallas_call(
        _attn_kernel,
        grid=(BATCH, N_KV_HEADS),
        in_specs=[
            pl.BlockSpec(memory_space=pltpu.SMEM),
            pl.BlockSpec((SEQ, qw), lambda b, h: (b, h)),
            pl.BlockSpec((SEQ, HEAD_DIM), lambda b, h: (b, k_off + h)),
            pl.BlockSpec((SEQ, HEAD_DIM), lambda b, h: (b, v_off + h)),
            pl.BlockSpec((CTX_LEN, HEAD_DIM), lambda b, h: (b, h)),
            pl.BlockSpec((CTX_LEN, HEAD_DIM), lambda b, h: (b, N_KV_HEADS + h)),
        ],
        out_specs=pl.BlockSpec((SEQ, qw), lambda b, h: (b, h)),
        out_shape=jax.ShapeDtypeStruct((T_LAT, N_HEADS * HEAD_DIM), BF16),
        scratch_shapes=[pltpu.VMEM((SEQ, 2 * HEAD_DIM), BF16), pltpu.VMEM((CTX_LEN, 2 * HEAD_DIM), BF16)],
        compiler_params=_params(("parallel", "parallel")),
        name="window_attention",
    )(sink, qkv, qkv, qkv, kvc, kvc)


def _rope_tables():
    rows_n = SEQ // GRID_W
    row = jnp.repeat(jnp.arange(rows_n), GRID_W).astype(F32)
    col = jnp.tile(jnp.arange(GRID_W), rows_n).astype(F32)
    nf = HEAD_DIM // 4
    inv_freq = ROPE_BASE ** (-jnp.arange(nf, dtype=F32) / nf)
    ang_r = row[:, None] * inv_freq[None, :]
    ang_c = col[:, None] * inv_freq[None, :]
    ang = jnp.concatenate([ang_r, ang_r, ang_c, ang_c], axis=-1)
    cos, sin = jnp.cos(ang), jnp.sin(ang)
    low_half = (jnp.arange(HEAD_DIM) % (2 * nf)) < nf
    sin_lo = jnp.where(low_half[None, :], -sin, 0.0)
    sin_hi = jnp.where(low_half[None, :], 0.0, sin)
    scale = LOG2E / math.sqrt(HEAD_DIM)
    ones, zeros = jnp.ones_like(cos), jnp.zeros_like(cos)
    cos3 = jnp.stack([cos * scale, cos, ones])
    lo3 = jnp.stack([sin_lo * scale, sin_lo, zeros])
    hi3 = jnp.stack([sin_hi * scale, sin_hi, zeros])
    return cos3, lo3, hi3


def _mlp(h, u2, mod, w1, w2_bf16, rows):
    a = _matmul_ws(u2, [(w1, 0)], [], [(D_FF, BF16)], _epi_relu2,
                   rows=rows, tm=2048, tn=1024, n_cols=D_FF, name="mlp_up")[0]
    return _matmul_ks_residual(a, w2_bf16, h, mod, 5, rows=rows, tm=1024, tn=1024, tk=2048, name="mlp_down")


def _out_proj(h, v, mod, w_out, rows):
    tm, tn = 1024, 1024
    return _matmul_ws(v, [(w_out, 0)], _residual_extras(h, mod, 2, tm, tn), [(D_MODEL, F32)], _epi_residual,
                      rows=rows, tm=tm, tn=tn, n_cols=D_MODEL, name="out_proj")[0]


def kernel(x, c, ctx, c_ctx, norm1_g, norm2_g, mod_w, mod_b, conv_w_in, conv_w, conv_w_out,
           attn_w_qkv, attn_sink, attn_w_o, mlp_w1, mlp_w2, final_g):
    h = jnp.concatenate([x.reshape(T_LAT, D_MODEL), ctx.reshape(T_CTX, D_MODEL)], axis=0)
    cvec = jnp.concatenate(
        [c, c_ctx[None, :], jnp.zeros((MOD_ROWS - BATCH - 1, D_MODEL), F32)], axis=0)
    mod = _mod_table(cvec, mod_w, mod_b)
    w2 = mlp_w2.astype(BF16)

    m0 = mod[0]
    u = _norm_mod(h, norm1_g[0], m0, 0, 1, T_ALL)
    w_in = conv_w_in[0]
    nblk = D_MODEL // 512
    b_gate, z = _matmul_ws(u, [(w_in, 0), (w_in, nblk), (w_in, 2 * nblk)], [],
                           [(D_MODEL, BF16), (D_MODEL, BF16)], _epi_gate,
                           rows=T_ALL, tm=1024, tn=512, n_cols=D_MODEL, name="conv_in_proj")
    v = _conv_gate(b_gate, z, conv_w[0], T_ALL)
    h = _out_proj(h, v, m0, conv_w_out[0], T_ALL)
    u2 = _norm_mod(h, norm2_g[0], m0, 3, 4, T_ALL)
    h = _mlp(h, u2, m0, mlp_w1[0], w2[0], T_ALL)

    m1 = mod[1]
    u = _norm_mod(h, norm1_g[1], m1, 0, 1, T_ALL)
    w_qkv = attn_w_qkv[0]
    cos3, lo3, hi3 = _rope_tables()
    tm, tn = 1024, 512
    pos_blocks = SEQ // tm
    qcols = N_HEADS * HEAD_DIM // tn

    def table_map(i, j):
        return (jnp.clip(j - (qcols - 1), 0, 2), i % pos_blocks, 0)

    rope_extras = [(t, (None, tm, HEAD_DIM), table_map) for t in (cos3, lo3, hi3)]
    qkv = _matmul_ws(u, [(w_qkv, 0)], rope_extras, [(QKV_W, BF16)], _epi_rope,
                     rows=T_LAT, tm=tm, tn=tn, n_cols=QKV_W, row_chunks=4, name="qkv_latent")[0]
    kv_cols = 2 * N_KV_HEADS * HEAD_DIM
    kvc = _matmul_ws(u, [(w_qkv, N_HEADS * HEAD_DIM // 512)], [], [(kv_cols, BF16)], _epi_plain,
                     rows=T_CTX, tm=1024, tn=512, n_cols=kv_cols,
                     row_blk_off=T_LAT // 1024, name="kv_context")[0]
    o = _attention(qkv, kvc, attn_sink[0])
    h = _out_proj(h, o, m1, attn_w_o[0], T_LAT)
    u2 = _norm_mod(h, norm2_g[1], m1, 3, 4, T_LAT)
    h = _mlp(h, u2, m1, mlp_w1[1], w2[1], T_LAT)
    out = _final_norm(h, final_g, T_LAT)
    return out.reshape(BATCH, SEQ, D_MODEL)
```

```python
import functools
import math

import jax
import jax.numpy as jnp
from jax import lax
from jax.experimental import pallas as pl
from jax.experimental.pallas import tpu as pltpu

F32 = jnp.float32
BF16 = jnp.bfloat16

D_MODEL = 2048
BATCH = 8
SEQ = 2048
CTX_LEN = 256
GRID_W = 64
HEAD_DIM = 128
N_HEADS = 16
N_KV_HEADS = 4
GROUP = 4
WINDOW = 128
ROPE_BASE = 10000.0
D_FF = 4 * D_MODEL
N_MOD = 6
EPS = 1e-6
NEG = -1e30
LOG2E = math.log2(math.e)

T_LAT = BATCH * SEQ
T_CTX = BATCH * CTX_LEN
T_ALL = T_LAT + T_CTX
MOD_ROWS = 16
QKV_W = (N_HEADS + 2 * N_KV_HEADS) * HEAD_DIM
KWIN = 3 * WINDOW

VMEM_LIMIT = 56 * 1024 * 1024


def _params(sem):
    return pltpu.CompilerParams(dimension_semantics=sem, vmem_limit_bytes=VMEM_LIMIT)


def _mod_row(i, tm):
    return jnp.minimum((i * tm) // SEQ, BATCH)


def _mod_kernel(c_ref, w_ref, b_ref, o_ref):
    s = jax.nn.silu(c_ref[...]).astype(BF16)
    o_ref[...] = jnp.dot(s, w_ref[...].astype(BF16), preferred_element_type=F32) + b_ref[...]


def _mod_table(cvec, mod_w, mod_b):
    depth = mod_w.shape[0]
    n = N_MOD * D_MODEL
    tn = 1024
    out = pl.pallas_call(
        _mod_kernel,
        grid=(depth, n // tn),
        in_specs=[
            pl.BlockSpec((MOD_ROWS, D_MODEL), lambda l, j: (0, 0)),
            pl.BlockSpec((None, D_MODEL, tn), lambda l, j: (l, 0, j)),
            pl.BlockSpec((None, 1, tn), lambda l, j: (l, 0, j)),
        ],
        out_specs=pl.BlockSpec((None, MOD_ROWS, tn), lambda l, j: (l, 0, j)),
        out_shape=jax.ShapeDtypeStruct((depth, MOD_ROWS, n), F32),
        compiler_params=_params(("parallel", "parallel")),
        name="mod_table",
    )(cvec, mod_w, mod_b.reshape(depth, 1, n))
    return out.reshape(depth, MOD_ROWS, 1, n)


def _rms_mod(x, g_ref, sc_ref, sh_ref):
    y = x * lax.rsqrt(jnp.mean(x * x, axis=-1, keepdims=True) + EPS)
    y = y * g_ref[...]
    return y * (1.0 + sc_ref[...]) + sh_ref[...]


def _pick_stream(refs, tile, n_lat_tiles, fn):
    if len(refs) == 1:
        fn(refs[0])
        return
    pl.when(tile < n_lat_tiles)(lambda: fn(refs[0]))
    pl.when(tile >= n_lat_tiles)(lambda: fn(refs[1]))


def _stream_specs(n_src, tm, tile_of):
    n_lat = T_LAT // tm
    if n_src == 1:
        return [pl.BlockSpec((tm, D_MODEL), lambda *g: (tile_of(*g), 0))]
    return [
        pl.BlockSpec((tm, D_MODEL), lambda *g: (jnp.minimum(tile_of(*g), n_lat - 1), 0)),
        pl.BlockSpec((tm, D_MODEL), lambda *g: (jnp.maximum(tile_of(*g) - n_lat, 0), 0)),
    ]


def _norm_mod_kernel(*refs, n_lat_tiles):
    srcs = refs[:-4]
    g_ref, sc_ref, sh_ref, o_ref = refs[-4:]

    def emit(x_ref):
        o_ref[...] = _rms_mod(x_ref[...], g_ref, sc_ref, sh_ref).astype(o_ref.dtype)

    _pick_stream(srcs, pl.program_id(0), n_lat_tiles, emit)


def _norm_mod(srcs, g, mod, shift_blk, scale_blk, rows):
    tm = 512
    return pl.pallas_call(
        functools.partial(_norm_mod_kernel, n_lat_tiles=T_LAT // tm),
        grid=(rows // tm,),
        in_specs=_stream_specs(len(srcs), tm, lambda i: i) + [
            pl.BlockSpec((1, D_MODEL), lambda i: (0, 0)),
            pl.BlockSpec((None, 1, D_MODEL), lambda i: (_mod_row(i, tm), 0, scale_blk)),
            pl.BlockSpec((None, 1, D_MODEL), lambda i: (_mod_row(i, tm), 0, shift_blk)),
        ],
        out_specs=pl.BlockSpec((tm, D_MODEL), lambda i: (i, 0)),
        out_shape=jax.ShapeDtypeStruct((rows, D_MODEL), BF16),
        compiler_params=_params(("parallel",)),
        name="norm_mod",
    )(*srcs, g.reshape(1, D_MODEL), mod, mod)


def _final_norm_kernel(x_ref, g_ref, o_ref):
    x = x_ref[...]
    y = x * lax.rsqrt(jnp.mean(x * x, axis=-1, keepdims=True) + EPS)
    o_ref[...] = y * g_ref[...]


def _final_norm(h, g, rows):
    tm = 512
    return pl.pallas_call(
        _final_norm_kernel,
        grid=(rows // tm,),
        in_specs=[
            pl.BlockSpec((tm, D_MODEL), lambda i: (i, 0)),
            pl.BlockSpec((1, D_MODEL), lambda i: (0, 0)),
        ],
        out_specs=pl.BlockSpec((tm, D_MODEL), lambda i: (i, 0)),
        out_shape=jax.ShapeDtypeStruct((rows, D_MODEL), F32),
        compiler_params=_params(("parallel",)),
        name="final_norm",
    )(h, g.reshape(1, D_MODEL))


def _ws_kernel(*refs, nw, ne, no, row_chunks, epilogue):
    x_ref = refs[0]
    w_refs = refs[1:1 + nw]
    e_refs = refs[1 + nw:1 + nw + ne]
    o_refs = refs[1 + nw + ne:1 + nw + ne + no]
    strip_refs = refs[1 + nw + ne + no:]

    @pl.when(pl.program_id(1) == 0)
    def _():
        for w_ref, strip in zip(w_refs, strip_refs):
            strip[...] = w_ref[...].astype(BF16)

    rc = x_ref.shape[0] // row_chunks
    for r in range(row_chunks):
        rows = pl.ds(r * rc, rc)
        x = x_ref[rows, :]
        accs = [jnp.dot(x, strip[...], preferred_element_type=F32) for strip in strip_refs]
        for o_ref, val in zip(o_refs, epilogue(accs, e_refs, rows)):
            o_ref[rows, :] = val.astype(o_ref.dtype)


def _matmul_ws(x, ws, extras, outs, epilogue, *, rows, tm, tn, n_cols, row_blk_off=0, row_chunks=1, name):
    kdim = x.shape[1]
    in_specs = [pl.BlockSpec((tm, kdim), lambda j, i: (i + row_blk_off, 0))]
    args = [x]
    for w, layer, off in ws:
        in_specs.append(pl.BlockSpec((None, kdim, tn), lambda j, i, layer=layer, off=off: (layer, 0, j + off)))
        args.append(w)
    for arr, blk, imap in extras:
        in_specs.append(pl.BlockSpec(blk, lambda j, i, imap=imap: imap(i, j)))
        args.append(arr)
    body = functools.partial(_ws_kernel, nw=len(ws), ne=len(extras), no=len(outs),
                             row_chunks=row_chunks, epilogue=epilogue)
    return pl.pallas_call(
        body,
        grid=(n_cols // tn, rows // tm),
        in_specs=in_specs,
        out_specs=[pl.BlockSpec((tm, tn), lambda j, i: (i, j)) for _ in outs],
        out_shape=[jax.ShapeDtypeStruct((rows, cols), dt) for cols, dt in outs],
        scratch_shapes=[pltpu.VMEM((kdim, tn), BF16) for _ in ws],
        compiler_params=_params(("arbitrary", "arbitrary")),
        name=name,
    )(*args)


def _ks_kernel(x_ref, w_ref, h_ref, gate_ref, o_ref, acc_ref, *, nk):
    k = pl.program_id(2)

    @pl.when(k == 0)
    def _():
        acc_ref[...] = jnp.dot(x_ref[...], w_ref[...], preferred_element_type=F32)

    @pl.when(k > 0)
    def _():
        acc_ref[...] += jnp.dot(x_ref[...], w_ref[...], preferred_element_type=F32)

    @pl.when(k == nk - 1)
    def _():
        o_ref[...] = h_ref[...] + gate_ref[...] * acc_ref[...]


def _matmul_ks_residual(x, w, layer, h, mod, gate_blk, *, rows, tm, tn, tk, name):
    _, kdim, n_cols = w.shape
    nk = kdim // tk
    per = D_MODEL // tn
    return pl.pallas_call(
        functools.partial(_ks_kernel, nk=nk),
        grid=(rows // tm, n_cols // tn, nk),
        in_specs=[
            pl.BlockSpec((tm, tk), lambda i, j, k: (i, k)),
            pl.BlockSpec((None, tk, tn), lambda i, j, k: (layer, k, j)),
            pl.BlockSpec((tm, tn), lambda i, j, k: (i, j)),
            pl.BlockSpec((None, 1, tn), lambda i, j, k: (_mod_row(i, tm), 0, gate_blk * per + j)),
        ],
        out_specs=pl.BlockSpec((tm, tn), lambda i, j, k: (i, j)),
        out_shape=jax.ShapeDtypeStruct((rows, n_cols), F32),
        scratch_shapes=[pltpu.VMEM((tm, tn), F32)],
        compiler_params=_params(("parallel", "parallel", "arbitrary")),
        name=name,
    )(x, w, h, mod)


def _epi_gate(accs, e_refs, rows):
    b_gate, c_gate, hval = accs
    return [b_gate, c_gate * hval]


def _epi_relu2(accs, e_refs, rows):
    a = jnp.maximum(accs[0], 0.0)
    return [a * a]


def _epi_plain(accs, e_refs, rows):
    return [accs[0]]


def _epi_rope(accs, e_refs, rows):
    cos_ref, sin_lo_ref, sin_hi_ref = e_refs
    acc = accs[0]
    cos, sin_lo, sin_hi = cos_ref[rows, :], sin_lo_ref[rows, :], sin_hi_ref[rows, :]
    heads = []
    for hh in range(acc.shape[1] // HEAD_DIM):
        a = acc[:, hh * HEAD_DIM:(hh + 1) * HEAD_DIM]
        up = pltpu.roll(a, HEAD_DIM - HEAD_DIM // 4, 1)
        dn = pltpu.roll(a, HEAD_DIM // 4, 1)
        heads.append(a * cos + up * sin_lo + dn * sin_hi)
    return [jnp.concatenate(heads, axis=1)]


def _proj_norm_kernel(*refs, n_src, n_cast, n_lat_tiles, row_chunks):
    x_ref, w_ref = refs[:2]
    h_refs = refs[2:2 + n_src]
    gate_ref, g_ref, sc_ref, sh_ref, h_out, u_out, wb_ref = refs[2 + n_src:]
    step = pl.program_id(0)
    kc = w_ref.shape[0]
    rc = x_ref.shape[0] // row_chunks

    @pl.when(step < n_cast)
    def _():
        wb_ref[pl.ds(pl.multiple_of(step * kc, kc), kc), :] = w_ref[...].astype(BF16)

    def compute(h_ref):
        for r in range(row_chunks):
            rows = pl.ds(r * rc, rc)
            acc = jnp.dot(x_ref[rows, :], wb_ref[...], preferred_element_type=F32)
            h_new = h_ref[rows, :] + gate_ref[...] * acc
            h_out[rows, :] = h_new
            u_out[rows, :] = _rms_mod(h_new, g_ref, sc_ref, sh_ref).astype(u_out.dtype)

    @pl.when(step >= n_cast)
    def _():
        _pick_stream(h_refs, step - n_cast, n_lat_tiles, compute)


def _proj_norm(x, w, layer, h_srcs, mod, g, rows):
    tm = 512
    n_cast = 4
    kc = D_MODEL // n_cast

    def tile(s):
        return jnp.maximum(s - n_cast, 0)

    def mod_spec(blk):
        return pl.BlockSpec((None, 1, D_MODEL), lambda s: (_mod_row(tile(s), tm), 0, blk))

    body = functools.partial(_proj_norm_kernel, n_src=len(h_srcs), n_cast=n_cast,
                             n_lat_tiles=T_LAT // tm, row_chunks=2)
    return pl.pallas_call(
        body,
        grid=(n_cast + rows // tm,),
        in_specs=[
            pl.BlockSpec((tm, D_MODEL), lambda s: (tile(s), 0)),
            pl.BlockSpec((None, kc, D_MODEL), lambda s: (layer, jnp.minimum(s, n_cast - 1), 0)),
        ] + _stream_specs(len(h_srcs), tm, tile) + [
            mod_spec(2),
            pl.BlockSpec((1, D_MODEL), lambda s: (0, 0)),
            mod_spec(4),
            mod_spec(3),
        ],
        out_specs=[pl.BlockSpec((tm, D_MODEL), lambda s: (tile(s), 0)),
                   pl.BlockSpec((tm, D_MODEL), lambda s: (tile(s), 0))],
        out_shape=[jax.ShapeDtypeStruct((rows, D_MODEL), F32),
                   jax.ShapeDtypeStruct((rows, D_MODEL), BF16)],
        scratch_shapes=[pltpu.VMEM((D_MODEL, D_MODEL), BF16)],
        compiler_params=_params(("arbitrary",)),
        name="proj_norm",
    )(x, w, *h_srcs, mod, g.reshape(1, D_MODEL), mod, mod)


HALO = 16


def _conv_gate_kernel(b_ref, z_ref, zp_ref, zn_ref, cw_ref, o_ref, pad_ref, *, tm):
    i = pl.program_id(0)
    row0 = i * tm
    seq_len = jnp.where(row0 >= T_LAT, CTX_LEN, SEQ)
    pos = (row0 + lax.broadcasted_iota(jnp.int32, (tm, 1), 0)) & (seq_len - 1)
    pad_ref[pl.ds(8, tm), :] = z_ref[...].astype(F32)
    pad_ref[pl.ds(7, 1), :] = zp_ref[pl.ds(HALO - 1, 1), :].astype(F32)
    pad_ref[pl.ds(8 + tm, 1), :] = zn_ref[pl.ds(0, 1), :].astype(F32)
    z_prev = jnp.where(pos == 0, 0.0, pad_ref[pl.ds(7, tm), :])
    z_next = jnp.where(pos == seq_len - 1, 0.0, pad_ref[pl.ds(9, tm), :])
    z_mid = pad_ref[pl.ds(8, tm), :]
    conv = z_prev * cw_ref[pl.ds(0, 1), :] + z_mid * cw_ref[pl.ds(1, 1), :] + z_next * cw_ref[pl.ds(2, 1), :]
    o_ref[...] = (b_ref[...].astype(F32) * conv).astype(o_ref.dtype)


def _conv_gate(b_gate, z, conv_w, rows):
    tm = 256
    per = tm // HALO
    last = rows // HALO - 1
    z3 = z.reshape(rows // HALO, HALO, D_MODEL)
    return pl.pallas_call(
        functools.partial(_conv_gate_kernel, tm=tm),
        grid=(rows // tm,),
        in_specs=[
            pl.BlockSpec((tm, D_MODEL), lambda i: (i, 0)),
            pl.BlockSpec((tm, D_MODEL), lambda i: (i, 0)),
            pl.BlockSpec((None, HALO, D_MODEL), lambda i: (jnp.maximum(i * per - 1, 0), 0, 0)),
            pl.BlockSpec((None, HALO, D_MODEL), lambda i: (jnp.minimum((i + 1) * per, last), 0, 0)),
            pl.BlockSpec((3, D_MODEL), lambda i: (0, 0)),
        ],
        out_specs=pl.BlockSpec((tm, D_MODEL), lambda i: (i, 0)),
        out_shape=jax.ShapeDtypeStruct((rows, D_MODEL), BF16),
        scratch_shapes=[pltpu.VMEM((tm + 16, D_MODEL), F32)],
        compiler_params=_params(("parallel",)),
        name="conv_gate",
    )(b_gate, z, z3, z3, conv_w)


def _attn_kernel(sink_ref, q_ref, k_ref, v_ref, kc_ref, vc_ref, o_ref, va_ref, vca_ref):
    kvh = pl.program_id(1)
    va_ref[:, :HEAD_DIM] = v_ref[...]
    va_ref[:, HEAD_DIM:] = jnp.ones((SEQ, HEAD_DIM), BF16)
    vca_ref[:, :HEAD_DIM] = vc_ref[...]
    vca_ref[:, HEAD_DIM:] = jnp.ones((CTX_LEN, HEAD_DIM), BF16)
    kc = kc_ref[...]
    vca = vca_ref[...]
    rel = (lax.broadcasted_iota(jnp.int32, (WINDOW, KWIN), 0)
           - lax.broadcasted_iota(jnp.int32, (WINDOW, KWIN), 1))
    nt = (((1,), (1,)), ((), ()))

    def body(n, carry):
        q0 = pl.multiple_of(n * WINDOW, WINDOW)
        start = pl.multiple_of(jnp.clip((n - 1) * WINDOW, 0, SEQ - KWIN), WINDOW)
        valid = jnp.abs(rel + (q0 - start)) <= WINDOW
        qs = jnp.concatenate(
            [q_ref[pl.ds(q0, WINDOW), g * HEAD_DIM:(g + 1) * HEAD_DIM] for g in range(GROUP)], axis=0)
        kw = k_ref[pl.ds(start, KWIN), :]
        vwa = va_ref[pl.ds(start, KWIN), :]
        s_loc = lax.dot_general(qs, kw, nt, preferred_element_type=F32)
        s_ctx = lax.dot_general(qs, kc, nt, preferred_element_type=F32)
        p_loc, p_ctx, sink_terms = [], [], []
        for g in range(GROUP):
            sl = jnp.where(valid, s_loc[g * WINDOW:(g + 1) * WINDOW], NEG)
            sc = s_ctx[g * WINDOW:(g + 1) * WINDOW]
            sk = sink_ref[kvh * GROUP + g] * LOG2E
            m = jnp.maximum(jnp.maximum(jnp.max(sl, axis=-1, keepdims=True),
                                        jnp.max(sc, axis=-1, keepdims=True)), sk)
            p_loc.append(jnp.exp2(sl - m).astype(BF16))
            p_ctx.append(jnp.exp2(sc - m).astype(BF16))
            sink_terms.append(jnp.exp2(sk - m))
        oa = (jnp.dot(jnp.concatenate(p_loc, axis=0), vwa, preferred_element_type=F32)
              + jnp.dot(jnp.concatenate(p_ctx, axis=0), vca, preferred_element_type=F32))
        o = oa[:, :HEAD_DIM] / (oa[:, HEAD_DIM:] + jnp.concatenate(sink_terms, axis=0))
        for g in range(GROUP):
            o_ref[pl.ds(q0, WINDOW), g * HEAD_DIM:(g + 1) * HEAD_DIM] = (
                o[g * WINDOW:(g + 1) * WINDOW].astype(o_ref.dtype))
        return carry

    lax.fori_loop(0, SEQ // WINDOW, body, 0, unroll=8)


def _attention(qkv, kvc, sink):
    qw = GROUP * HEAD_DIM
    k_off = N_HEADS
    v_off = N_HEADS + N_KV_HEADS
    return pl.pallas_call(
        _attn_kernel,
        grid=(BATCH, N_KV_HEADS),
        in_specs=[
            pl.BlockSpec(memory_space=pltpu.SMEM),
            pl.BlockSpec((SEQ, qw), lambda b, h: (b, h)),
            pl.BlockSpec((SEQ, HEAD_DIM), lambda b, h: (b, k_off + h)),
            pl.BlockSpec((SEQ, HEAD_DIM), lambda b, h: (b, v_off + h)),
            pl.BlockSpec((CTX_LEN, HEAD_DIM), lambda b, h: (b, h)),
            pl.BlockSpec((CTX_LEN, HEAD_DIM), lambda b, h: (b, N_KV_HEADS + h)),
        ],
        out_specs=pl.BlockSpec((SEQ, qw), lambda b, h: (b, h)),
        out_shape=jax.ShapeDtypeStruct((T_LAT, N_HEADS * HEAD_DIM), BF16),
        scratch_shapes=[pltpu.VMEM((SEQ, 2 * HEAD_DIM), BF16), pltpu.VMEM((CTX_LEN, 2 * HEAD_DIM), BF16)],
        compiler_params=_params(("parallel", "parallel")),
        name="window_attention",
    )(sink, qkv, qkv, qkv, kvc, kvc)


def _rope_tables():
    rows_n = SEQ // GRID_W
    row = jnp.repeat(jnp.arange(rows_n), GRID_W).astype(F32)
    col = jnp.tile(jnp.arange(GRID_W), rows_n).astype(F32)
    nf = HEAD_DIM // 4
    inv_freq = ROPE_BASE ** (-jnp.arange(nf, dtype=F32) / nf)
    ang_r = row[:, None] * inv_freq[None, :]
    ang_c = col[:, None] * inv_freq[None, :]
    ang = jnp.concatenate([ang_r, ang_r, ang_c, ang_c], axis=-1)
    cos, sin = jnp.cos(ang), jnp.sin(ang)
    low_half = (jnp.arange(HEAD_DIM) % (2 * nf)) < nf
    sin_lo = jnp.where(low_half[None, :], -sin, 0.0)
    sin_hi = jnp.where(low_half[None, :], 0.0, sin)
    scale = LOG2E / math.sqrt(HEAD_DIM)
    ones, zeros = jnp.ones_like(cos), jnp.zeros_like(cos)
    cos3 = jnp.stack([cos * scale, cos, ones])
    lo3 = jnp.stack([sin_lo * scale, sin_lo, zeros])
    hi3 = jnp.stack([sin_hi * scale, sin_hi, zeros])
    return cos3, lo3, hi3


def _mlp(h, u2, mod, w1, w2_bf16, layer, rows):
    a = _matmul_ws(u2, [(w1, layer, 0)], [], [(D_FF, BF16)], _epi_relu2,
                   rows=rows, tm=2048, tn=1024, n_cols=D_FF, name="mlp_up")[0]
    return _matmul_ks_residual(a, w2_bf16, layer, h, mod, 5, rows=rows, tm=1024, tn=1024, tk=2048,
                               name="mlp_down")


def kernel(x, c, ctx, c_ctx, norm1_g, norm2_g, mod_w, mod_b, conv_w_in, conv_w, conv_w_out,
           attn_w_qkv, attn_sink, attn_w_o, mlp_w1, mlp_w2, final_g):
    x_lat = x.reshape(T_LAT, D_MODEL)
    x_ctx = ctx.reshape(T_CTX, D_MODEL)
    cvec = jnp.concatenate(
        [c, c_ctx[None, :], jnp.zeros((MOD_ROWS - BATCH - 1, D_MODEL), F32)], axis=0)
    mod = _mod_table(cvec, mod_w, mod_b)
    w2 = mlp_w2.astype(BF16)

    m0 = mod[0]
    u = _norm_mod([x_lat, x_ctx], norm1_g[0], m0, 0, 1, T_ALL)
    nblk = D_MODEL // 512
    b_gate, z = _matmul_ws(u, [(conv_w_in, 0, 0), (conv_w_in, 0, nblk), (conv_w_in, 0, 2 * nblk)], [],
                           [(D_MODEL, BF16), (D_MODEL, BF16)], _epi_gate,
                           rows=T_ALL, tm=1024, tn=512, n_cols=D_MODEL, name="conv_in_proj")
    v = _conv_gate(b_gate, z, conv_w[0], T_ALL)
    h, u2 = _proj_norm(v, conv_w_out, 0, [x_lat, x_ctx], m0, norm2_g[0], T_ALL)
    h = _mlp(h, u2, m0, mlp_w1, w2, 0, T_ALL)

    m1 = mod[1]
    u = _norm_mod([h], norm1_g[1], m1, 0, 1, T_ALL)
    w_qkv = attn_w_qkv
    cos3, lo3, hi3 = _rope_tables()
    tm, tn = 1024, 512
    pos_blocks = SEQ // tm
    qcols = N_HEADS * HEAD_DIM // tn

    def table_map(i, j):
        return (jnp.clip(j - (qcols - 1), 0, 2), i % pos_blocks, 0)

    rope_extras = [(t, (None, tm, HEAD_DIM), table_map) for t in (cos3, lo3, hi3)]
    qkv = _matmul_ws(u, [(w_qkv, 0, 0)], rope_extras, [(QKV_W, BF16)], _epi_rope,
                     rows=T_LAT, tm=tm, tn=tn, n_cols=QKV_W, row_chunks=4, name="qkv_latent")[0]
    kv_cols = 2 * N_KV_HEADS * HEAD_DIM
    kvc = _matmul_ws(u, [(w_qkv, 0, N_HEADS * HEAD_DIM // 512)], [], [(kv_cols, BF16)], _epi_plain,
                     rows=T_CTX, tm=1024, tn=512, n_cols=kv_cols,
                     row_blk_off=T_LAT // 1024, name="kv_context")[0]
    o = _attention(qkv, kvc, attn_sink[0])
    h, u2 = _proj_norm(o, attn_w_o, 0, [h], m1, norm2_g[1], T_LAT)
    h = _mlp(h, u2, m1, mlp_w1, w2, 1, T_LAT)
    out = _final_norm(h, final_g, T_LAT)
    return out.reshape(BATCH, SEQ, D_MODEL)
```

```python
import functools
import math

import jax
import jax.numpy as jnp
from jax import lax
from jax.experimental import pallas as pl
from jax.experimental.pallas import tpu as pltpu

F32 = jnp.float32
BF16 = jnp.bfloat16

D_MODEL = 2048
BATCH = 8
SEQ = 2048
CTX_LEN = 256
GRID_W = 64
HEAD_DIM = 128
N_HEADS = 16
N_KV_HEADS = 4
GROUP = 4
WINDOW = 128
ROPE_BASE = 10000.0
D_FF = 4 * D_MODEL
N_MOD = 6
EPS = 1e-6
NEG = -1e30
LOG2E = math.log2(math.e)

T_LAT = BATCH * SEQ
T_CTX = BATCH * CTX_LEN
T_ALL = T_LAT + T_CTX
MOD_ROWS = 16
QKV_W = (N_HEADS + 2 * N_KV_HEADS) * HEAD_DIM
KWIN = 3 * WINDOW

VMEM_LIMIT = 56 * 1024 * 1024


def _params(sem):
    return pltpu.CompilerParams(dimension_semantics=sem, vmem_limit_bytes=VMEM_LIMIT)


def _mod_row(i, tm):
    return jnp.minimum((i * tm) // SEQ, BATCH)


def _mod_kernel(c_ref, w_ref, b_ref, o_ref):
    s = jax.nn.silu(c_ref[...]).astype(BF16)
    o_ref[...] = jnp.dot(s, w_ref[...].astype(BF16), preferred_element_type=F32) + b_ref[...]


def _mod_table(cvec, mod_w, mod_b):
    depth = mod_w.shape[0]
    n = N_MOD * D_MODEL
    tn = 1024
    out = pl.pallas_call(
        _mod_kernel,
        grid=(depth, n // tn),
        in_specs=[
            pl.BlockSpec((MOD_ROWS, D_MODEL), lambda l, j: (0, 0)),
            pl.BlockSpec((None, D_MODEL, tn), lambda l, j: (l, 0, j)),
            pl.BlockSpec((None, 1, tn), lambda l, j: (l, 0, j)),
        ],
        out_specs=pl.BlockSpec((None, MOD_ROWS, tn), lambda l, j: (l, 0, j)),
        out_shape=jax.ShapeDtypeStruct((depth, MOD_ROWS, n), F32),
        compiler_params=_params(("parallel", "parallel")),
        name="mod_table",
    )(cvec, mod_w, mod_b.reshape(depth, 1, n))
    return out.reshape(depth, MOD_ROWS, 1, n)


def _rms_mod(x, g_ref, sc_ref, sh_ref):
    y = x * lax.rsqrt(jnp.mean(x * x, axis=-1, keepdims=True) + EPS)
    y = y * g_ref[...]
    return y * (1.0 + sc_ref[...]) + sh_ref[...]


def _pick_stream(refs, tile, n_lat_tiles, fn):
    if len(refs) == 1:
        fn(refs[0])
        return
    pl.when(tile < n_lat_tiles)(lambda: fn(refs[0]))
    pl.when(tile >= n_lat_tiles)(lambda: fn(refs[1]))


def _stream_specs(n_src, tm, tile_of):
    n_lat = T_LAT // tm
    if n_src == 1:
        return [pl.BlockSpec((tm, D_MODEL), lambda *g: (tile_of(*g), 0))]
    return [
        pl.BlockSpec((tm, D_MODEL), lambda *g: (jnp.minimum(tile_of(*g), n_lat - 1), 0)),
        pl.BlockSpec((tm, D_MODEL), lambda *g: (jnp.maximum(tile_of(*g) - n_lat, 0), 0)),
    ]


def _norm_mod_kernel(*refs, n_lat_tiles):
    srcs = refs[:-4]
    g_ref, sc_ref, sh_ref, o_ref = refs[-4:]

    def emit(x_ref):
        o_ref[...] = _rms_mod(x_ref[...], g_ref, sc_ref, sh_ref).astype(o_ref.dtype)

    _pick_stream(srcs, pl.program_id(0), n_lat_tiles, emit)


def _norm_mod(srcs, g, mod, shift_blk, scale_blk, rows):
    tm = 512
    return pl.pallas_call(
        functools.partial(_norm_mod_kernel, n_lat_tiles=T_LAT // tm),
        grid=(rows // tm,),
        in_specs=_stream_specs(len(srcs), tm, lambda i: i) + [
            pl.BlockSpec((1, D_MODEL), lambda i: (0, 0)),
            pl.BlockSpec((None, 1, D_MODEL), lambda i: (_mod_row(i, tm), 0, scale_blk)),
            pl.BlockSpec((None, 1, D_MODEL), lambda i: (_mod_row(i, tm), 0, shift_blk)),
        ],
        out_specs=pl.BlockSpec((tm, D_MODEL), lambda i: (i, 0)),
        out_shape=jax.ShapeDtypeStruct((rows, D_MODEL), BF16),
        compiler_params=_params(("parallel",)),
        name="norm_mod",
    )(*srcs, g.reshape(1, D_MODEL), mod, mod)


def _ws_kernel(*refs, nw, ne, no, row_chunks, epilogue):
    x_ref = refs[0]
    w_refs = refs[1:1 + nw]
    e_refs = refs[1 + nw:1 + nw + ne]
    o_refs = refs[1 + nw + ne:1 + nw + ne + no]
    strip_refs = refs[1 + nw + ne + no:]

    @pl.when(pl.program_id(1) == 0)
    def _():
        for w_ref, strip in zip(w_refs, strip_refs):
            strip[...] = w_ref[...].astype(BF16)

    rc = x_ref.shape[0] // row_chunks
    for r in range(row_chunks):
        rows = pl.ds(r * rc, rc)
        x = x_ref[rows, :]
        accs = [jnp.dot(x, strip[...], preferred_element_type=F32) for strip in strip_refs]
        for o_ref, val in zip(o_refs, epilogue(accs, e_refs, rows)):
            o_ref[rows, :] = val.astype(o_ref.dtype)


def _matmul_ws(x, ws, extras, outs, epilogue, *, rows, tm, tn, n_cols, row_blk_off=0, row_chunks=1, name):
    kdim = x.shape[1]
    in_specs = [pl.BlockSpec((tm, kdim), lambda j, i: (i + row_blk_off, 0))]
    args = [x]
    for w, layer, off in ws:
        in_specs.append(pl.BlockSpec((None, kdim, tn), lambda j, i, layer=layer, off=off: (layer, 0, j + off)))
        args.append(w)
    for arr, blk, imap in extras:
        in_specs.append(pl.BlockSpec(blk, lambda j, i, imap=imap: imap(i, j)))
        args.append(arr)
    body = functools.partial(_ws_kernel, nw=len(ws), ne=len(extras), no=len(outs),
                             row_chunks=row_chunks, epilogue=epilogue)
    return pl.pallas_call(
        body,
        grid=(n_cols // tn, rows // tm),
        in_specs=in_specs,
        out_specs=[pl.BlockSpec((tm, tn), lambda j, i: (i, j)) for _ in outs],
        out_shape=[jax.ShapeDtypeStruct((rows, cols), dt) for cols, dt in outs],
        scratch_shapes=[pltpu.VMEM((kdim, tn), BF16) for _ in ws],
        compiler_params=_params(("arbitrary", "arbitrary")),
        name=name,
    )(*args)


def _mlp_down_kernel(*refs, nk, final, row_chunks):
    if final:
        x_ref, w_ref, h_ref, gate_ref, g_ref, out_ref, acc_ref, keep_ref = refs
        sc_ref = sh_ref = h_out = None
    else:
        x_ref, w_ref, h_ref, gate_ref, g_ref, sc_ref, sh_ref, h_out, out_ref, acc_ref, keep_ref = refs
    j = pl.program_id(1)
    k = pl.program_id(2)
    tm, tn = acc_ref.shape
    rc = tm // row_chunks

    @pl.when(k == 0)
    def _():
        acc_ref[...] = jnp.dot(x_ref[...], w_ref[...], preferred_element_type=F32)

    @pl.when(jnp.logical_and(k > 0, k < nk - 1))
    def _():
        acc_ref[...] += jnp.dot(x_ref[...], w_ref[...], preferred_element_type=F32)

    def h_new_rows(rows):
        acc = acc_ref[rows, :] + jnp.dot(x_ref[rows, :], w_ref[...], preferred_element_type=F32)
        h_new = h_ref[rows, :] + gate_ref[...] * acc
        if not final:
            h_out[rows, :] = h_new
        return h_new

    @pl.when(jnp.logical_and(k == nk - 1, j == 0))
    def _():
        for r in range(row_chunks):
            rows = pl.ds(r * rc, rc)
            keep_ref[rows, :] = h_new_rows(rows)

    @pl.when(jnp.logical_and(k == nk - 1, j == 1))
    def _():
        for r in range(row_chunks):
            rows = pl.ds(r * rc, rc)
            halves = (keep_ref[rows, :], h_new_rows(rows))
            ssq = sum(jnp.sum(hv * hv, axis=-1, keepdims=True) for hv in halves)
            rstd = lax.rsqrt(ssq * (1.0 / D_MODEL) + EPS)
            for half, hv in enumerate(halves):
                cols = slice(half * tn, (half + 1) * tn)
                y = hv * rstd * g_ref[:, cols]
                if not final:
                    y = y * (1.0 + sc_ref[:, cols]) + sh_ref[:, cols]
                out_ref[rows, cols] = y.astype(out_ref.dtype)


def _mlp_down(x, w, layer, h, mod, g, mod_next, *, rows):
    tm, tk = 1024, 2048
    tn = D_MODEL // 2
    nk = D_FF // tk
    final = mod_next is None

    def mod_spec(blk, width):
        return pl.BlockSpec((None, 1, width), lambda i, j, k: (_mod_row(i, tm), 0, blk))

    in_specs = [
        pl.BlockSpec((tm, tk), lambda i, j, k: (i, k)),
        pl.BlockSpec((None, tk, tn), lambda i, j, k: (layer, k, j)),
        pl.BlockSpec((tm, tn), lambda i, j, k: (i, j)),
        pl.BlockSpec((None, 1, tn), lambda i, j, k: (_mod_row(i, tm), 0, 5 * 2 + j)),
        pl.BlockSpec((1, D_MODEL), lambda i, j, k: (0, 0)),
    ]
    args = [x, w, h, mod, g.reshape(1, D_MODEL)]
    full_rows = pl.BlockSpec((tm, D_MODEL), lambda i, j, k: (i, 0))
    if final:
        out_specs = [full_rows]
        out_shape = [jax.ShapeDtypeStruct((rows, D_MODEL), F32)]
    else:
        in_specs += [mod_spec(1, D_MODEL), mod_spec(0, D_MODEL)]
        args += [mod_next, mod_next]
        out_specs = [pl.BlockSpec((tm, tn), lambda i, j, k: (i, j)), full_rows]
        out_shape = [jax.ShapeDtypeStruct((rows, D_MODEL), F32), jax.ShapeDtypeStruct((rows, D_MODEL), BF16)]
    return pl.pallas_call(
        functools.partial(_mlp_down_kernel, nk=nk, final=final, row_chunks=4),
        grid=(rows // tm, 2, nk),
        in_specs=in_specs,
        out_specs=out_specs,
        out_shape=out_shape,
        scratch_shapes=[pltpu.VMEM((tm, tn), F32), pltpu.VMEM((tm, tn), F32)],
        compiler_params=_params(("arbitrary", "arbitrary", "arbitrary")),
        name="mlp_down",
    )(*args)


def _epi_gate(accs, e_refs, rows):
    b_gate, c_gate, hval = accs
    return [b_gate, c_gate * hval]


def _epi_relu2(accs, e_refs, rows):
    a = jnp.maximum(accs[0], 0.0)
    return [a * a]


def _epi_plain(accs, e_refs, rows):
    return [accs[0]]


def _epi_rope(accs, e_refs, rows):
    cos_ref, sin_lo_ref, sin_hi_ref = e_refs
    acc = accs[0]
    cos, sin_lo, sin_hi = cos_ref[rows, :], sin_lo_ref[rows, :], sin_hi_ref[rows, :]
    heads = []
    for hh in range(acc.shape[1] // HEAD_DIM):
        a = acc[:, hh * HEAD_DIM:(hh + 1) * HEAD_DIM]
        up = pltpu.roll(a, HEAD_DIM - HEAD_DIM // 4, 1)
        dn = pltpu.roll(a, HEAD_DIM // 4, 1)
        heads.append(a * cos + up * sin_lo + dn * sin_hi)
    return [jnp.concatenate(heads, axis=1)]


def _proj_norm_kernel(*refs, n_src, n_cast, n_lat_tiles, row_chunks):
    x_ref, w_ref = refs[:2]
    h_refs = refs[2:2 + n_src]
    gate_ref, g_ref, sc_ref, sh_ref, h_out, u_out, wb_ref = refs[2 + n_src:]
    step = pl.program_id(0)
    kc = w_ref.shape[0]
    rc = x_ref.shape[0] // row_chunks

    @pl.when(step < n_cast)
    def _():
        wb_ref[pl.ds(pl.multiple_of(step * kc, kc), kc), :] = w_ref[...].astype(BF16)

    def compute(h_ref):
        for r in range(row_chunks):
            rows = pl.ds(r * rc, rc)
            acc = jnp.dot(x_ref[rows, :], wb_ref[...], preferred_element_type=F32)
            h_new = h_ref[rows, :] + gate_ref[...] * acc
            h_out[rows, :] = h_new
            u_out[rows, :] = _rms_mod(h_new, g_ref, sc_ref, sh_ref).astype(u_out.dtype)

    @pl.when(step >= n_cast)
    def _():
        _pick_stream(h_refs, step - n_cast, n_lat_tiles, compute)


def _proj_norm(x, w, layer, h_srcs, mod, g, rows):
    tm = 512
    n_cast = 4
    kc = D_MODEL // n_cast

    def tile(s):
        return jnp.maximum(s - n_cast, 0)

    def mod_spec(blk):
        return pl.BlockSpec((None, 1, D_MODEL), lambda s: (_mod_row(tile(s), tm), 0, blk))

    body = functools.partial(_proj_norm_kernel, n_src=len(h_srcs), n_cast=n_cast,
                             n_lat_tiles=T_LAT // tm, row_chunks=2)
    return pl.pallas_call(
        body,
        grid=(n_cast + rows // tm,),
        in_specs=[
            pl.BlockSpec((tm, D_MODEL), lambda s: (tile(s), 0)),
            pl.BlockSpec((None, kc, D_MODEL), lambda s: (layer, jnp.minimum(s, n_cast - 1), 0)),
        ] + _stream_specs(len(h_srcs), tm, tile) + [
            mod_spec(2),
            pl.BlockSpec((1, D_MODEL), lambda s: (0, 0)),
            mod_spec(4),
            mod_spec(3),
        ],
        out_specs=[pl.BlockSpec((tm, D_MODEL), lambda s: (tile(s), 0)),
                   pl.BlockSpec((tm, D_MODEL), lambda s: (tile(s), 0))],
        out_shape=[jax.ShapeDtypeStruct((rows, D_MODEL), F32),
                   jax.ShapeDtypeStruct((rows, D_MODEL), BF16)],
        scratch_shapes=[pltpu.VMEM((D_MODEL, D_MODEL), BF16)],
        compiler_params=_params(("arbitrary",)),
        name="proj_norm",
    )(x, w, *h_srcs, mod, g.reshape(1, D_MODEL), mod, mod)


HALO = 16


def _conv_gate_kernel(b_ref, z_ref, zp_ref, zn_ref, cw_ref, o_ref, pad_ref, *, tm):
    i = pl.program_id(0)
    row0 = i * tm
    seq_len = jnp.where(row0 >= T_LAT, CTX_LEN, SEQ)
    pos = (row0 + lax.broadcasted_iota(jnp.int32, (tm, 1), 0)) & (seq_len - 1)
    pad_ref[pl.ds(8, tm), :] = z_ref[...].astype(F32)
    pad_ref[pl.ds(7, 1), :] = zp_ref[pl.ds(HALO - 1, 1), :].astype(F32)
    pad_ref[pl.ds(8 + tm, 1), :] = zn_ref[pl.ds(0, 1), :].astype(F32)
    z_prev = jnp.where(pos == 0, 0.0, pad_ref[pl.ds(7, tm), :])
    z_next = jnp.where(pos == seq_len - 1, 0.0, pad_ref[pl.ds(9, tm), :])
    z_mid = pad_ref[pl.ds(8, tm), :]
    conv = z_prev * cw_ref[pl.ds(0, 1), :] + z_mid * cw_ref[pl.ds(1, 1), :] + z_next * cw_ref[pl.ds(2, 1), :]
    o_ref[...] = (b_ref[...].astype(F32) * conv).astype(o_ref.dtype)


def _conv_gate(b_gate, z, conv_w, rows):
    tm = 256
    per = tm // HALO
    last = rows // HALO - 1
    z3 = z.reshape(rows // HALO, HALO, D_MODEL)
    return pl.pallas_call(
        functools.partial(_conv_gate_kernel, tm=tm),
        grid=(rows // tm,),
        in_specs=[
            pl.BlockSpec((tm, D_MODEL), lambda i: (i, 0)),
            pl.BlockSpec((tm, D_MODEL), lambda i: (i, 0)),
            pl.BlockSpec((None, HALO, D_MODEL), lambda i: (jnp.maximum(i * per - 1, 0), 0, 0)),
            pl.BlockSpec((None, HALO, D_MODEL), lambda i: (jnp.minimum((i + 1) * per, last), 0, 0)),
            pl.BlockSpec((3, D_MODEL), lambda i: (0, 0)),
        ],
        out_specs=pl.BlockSpec((tm, D_MODEL), lambda i: (i, 0)),
        out_shape=jax.ShapeDtypeStruct((rows, D_MODEL), BF16),
        scratch_shapes=[pltpu.VMEM((tm + 16, D_MODEL), F32)],
        compiler_params=_params(("parallel",)),
        name="conv_gate",
    )(b_gate, z, z3, z3, conv_w)


def _attn_kernel(sink_ref, q_ref, k_ref, v_ref, kc_ref, vc_ref, o_ref, va_ref, vca_ref):
    kvh = pl.program_id(1)
    va_ref[:, :HEAD_DIM] = v_ref[...]
    va_ref[:, HEAD_DIM:] = jnp.ones((SEQ, HEAD_DIM), BF16)
    vca_ref[:, :HEAD_DIM] = vc_ref[...]
    vca_ref[:, HEAD_DIM:] = jnp.ones((CTX_LEN, HEAD_DIM), BF16)
    kc = kc_ref[...]
    vca = vca_ref[...]
    rel = (lax.broadcasted_iota(jnp.int32, (WINDOW, KWIN), 0)
           - lax.broadcasted_iota(jnp.int32, (WINDOW, KWIN), 1))
    nt = (((1,), (1,)), ((), ()))

    def body(n, carry):
        q0 = pl.multiple_of(n * WINDOW, WINDOW)
        start = pl.multiple_of(jnp.clip((n - 1) * WINDOW, 0, SEQ - KWIN), WINDOW)
        valid = jnp.abs(rel + (q0 - start)) <= WINDOW
        qs = jnp.concatenate(
            [q_ref[pl.ds(q0, WINDOW), g * HEAD_DIM:(g + 1) * HEAD_DIM] for g in range(GROUP)], axis=0)
        kw = k_ref[pl.ds(start, KWIN), :]
        vwa = va_ref[pl.ds(start, KWIN), :]
        s_loc = lax.dot_general(qs, kw, nt, preferred_element_type=F32)
        s_ctx = lax.dot_general(qs, kc, nt, preferred_element_type=F32)
        p_loc, p_ctx, sink_terms = [], [], []
        for g in range(GROUP):
            sl = jnp.where(valid, s_loc[g * WINDOW:(g + 1) * WINDOW], NEG)
            sc = s_ctx[g * WINDOW:(g + 1) * WINDOW]
            sk = sink_ref[kvh * GROUP + g] * LOG2E
            m = jnp.maximum(jnp.maximum(jnp.max(sl, axis=-1, keepdims=True),
                                        jnp.max(sc, axis=-1, keepdims=True)), sk)
            p_loc.append(jnp.exp2(sl - m).astype(BF16))
            p_ctx.append(jnp.exp2(sc - m).astype(BF16))
            sink_terms.append(jnp.exp2(sk - m))
        oa = (jnp.dot(jnp.concatenate(p_loc, axis=0), vwa, preferred_element_type=F32)
              + jnp.dot(jnp.concatenate(p_ctx, axis=0), vca, preferred_element_type=F32))
        o = oa[:, :HEAD_DIM] / (oa[:, HEAD_DIM:] + jnp.concatenate(sink_terms, axis=0))
        for g in range(GROUP):
            o_ref[pl.ds(q0, WINDOW), g * HEAD_DIM:(g + 1) * HEAD_DIM] = (
                o[g * WINDOW:(g + 1) * WINDOW].astype(o_ref.dtype))
        return carry

    lax.fori_loop(0, SEQ // WINDOW, body, 0, unroll=8)


def _attention(qkv, kvc, sink):
    qw = GROUP * HEAD_DIM
    k_off = N_HEADS
    v_off = N_HEADS + N_KV_HEADS
    return pl.pallas_call(
        _attn_kernel,
        grid=(BATCH, N_KV_HEADS),
        in_specs=[
            pl.BlockSpec(memory_space=pltpu.SMEM),
            pl.BlockSpec((SEQ, qw), lambda b, h: (b, h)),
            pl.BlockSpec((SEQ, HEAD_DIM), lambda b, h: (b, k_off + h)),
            pl.BlockSpec((SEQ, HEAD_DIM), lambda b, h: (b, v_off + h)),
            pl.BlockSpec((CTX_LEN, HEAD_DIM), lambda b, h: (b, h)),
            pl.BlockSpec((CTX_LEN, HEAD_DIM), lambda b, h: (b, N_KV_HEADS + h)),
        ],
        out_specs=pl.BlockSpec((SEQ, qw), lambda b, h: (b, h)),
        out_shape=jax.ShapeDtypeStruct((T_LAT, N_HEADS * HEAD_DIM), BF16),
        scratch_shapes=[pltpu.VMEM((SEQ, 2 * HEAD_DIM), BF16), pltpu.VMEM((CTX_LEN, 2 * HEAD_DIM), BF16)],
        compiler_params=_params(("parallel", "parallel")),
        name="window_attention",
    )(sink, qkv, qkv, qkv, kvc, kvc)


def _rope_tables():
    rows_n = SEQ // GRID_W
    row = jnp.repeat(jnp.arange(rows_n), GRID_W).astype(F32)
    col = jnp.tile(jnp.arange(GRID_W), rows_n).astype(F32)
    nf = HEAD_DIM // 4
    inv_freq = ROPE_BASE ** (-jnp.arange(nf, dtype=F32) / nf)
    ang_r = row[:, None] * inv_freq[None, :]
    ang_c = col[:, None] * inv_freq[None, :]
    ang = jnp.concatenate([ang_r, ang_r, ang_c, ang_c], axis=-1)
    cos, sin = jnp.cos(ang), jnp.sin(ang)
    low_half = (jnp.arange(HEAD_DIM) % (2 * nf)) < nf
    sin_lo = jnp.where(low_half[None, :], -sin, 0.0)
    sin_hi = jnp.where(low_half[None, :], 0.0, sin)
    scale = LOG2E / math.sqrt(HEAD_DIM)
    ones, zeros = jnp.ones_like(cos), jnp.zeros_like(cos)
    cos3 = jnp.stack([cos * scale, cos, ones])
    lo3 = jnp.stack([sin_lo * scale, sin_lo, zeros])
    hi3 = jnp.stack([sin_hi * scale, sin_hi, zeros])
    return cos3, lo3, hi3


def _mlp(h, u2, mod, w1, w2_bf16, layer, g_next, mod_next, rows):
    a = _matmul_ws(u2, [(w1, layer, 0)], [], [(D_FF, BF16)], _epi_relu2,
                   rows=rows, tm=2048, tn=1024, n_cols=D_FF, name="mlp_up")[0]
    return _mlp_down(a, w2_bf16, layer, h, mod, g_next, mod_next, rows=rows)


def kernel(x, c, ctx, c_ctx, norm1_g, norm2_g, mod_w, mod_b, conv_w_in, conv_w, conv_w_out,
           attn_w_qkv, attn_sink, attn_w_o, mlp_w1, mlp_w2, final_g):
    x_lat = x.reshape(T_LAT, D_MODEL)
    x_ctx = ctx.reshape(T_CTX, D_MODEL)
    cvec = jnp.concatenate(
        [c, c_ctx[None, :], jnp.zeros((MOD_ROWS - BATCH - 1, D_MODEL), F32)], axis=0)
    mod = _mod_table(cvec, mod_w, mod_b)
    w2 = mlp_w2.astype(BF16)

    m0 = mod[0]
    u = _norm_mod([x_lat, x_ctx], norm1_g[0], m0, 0, 1, T_ALL)
    nblk = D_MODEL // 512
    b_gate, z = _matmul_ws(u, [(conv_w_in, 0, 0), (conv_w_in, 0, nblk), (conv_w_in, 0, 2 * nblk)], [],
                           [(D_MODEL, BF16), (D_MODEL, BF16)], _epi_gate,
                           rows=T_ALL, tm=1024, tn=512, n_cols=D_MODEL, name="conv_in_proj")
    v = _conv_gate(b_gate, z, conv_w[0], T_ALL)
    h, u2 = _proj_norm(v, conv_w_out, 0, [x_lat, x_ctx], m0, norm2_g[0], T_ALL)
    m1 = mod[1]
    h, u = _mlp(h, u2, m0, mlp_w1, w2, 0, norm1_g[1], m1, T_ALL)

    w_qkv = attn_w_qkv
    cos3, lo3, hi3 = _rope_tables()
    tm, tn = 1024, 512
    pos_blocks = SEQ // tm
    qcols = N_HEADS * HEAD_DIM // tn

    def table_map(i, j):
        return (jnp.clip(j - (qcols - 1), 0, 2), i % pos_blocks, 0)

    rope_extras = [(t, (None, tm, HEAD_DIM), table_map) for t in (cos3, lo3, hi3)]
    qkv = _matmul_ws(u, [(w_qkv, 0, 0)], rope_extras, [(QKV_W, BF16)], _epi_rope,
                     rows=T_LAT, tm=tm, tn=tn, n_cols=QKV_W, row_chunks=4, name="qkv_latent")[0]
    kv_cols = 2 * N_KV_HEADS * HEAD_DIM
    kvc = _matmul_ws(u, [(w_qkv, 0, N_HEADS * HEAD_DIM // 512)], [], [(kv_cols, BF16)], _epi_plain,
                     rows=T_CTX, tm=1024, tn=512, n_cols=kv_cols,
                     row_blk_off=T_LAT // 1024, name="kv_context")[0]
    o = _attention(qkv, kvc, attn_sink[0])
    h, u2 = _proj_norm(o, attn_w_o, 0, [h], m1, norm2_g[1], T_LAT)
    out = _mlp(h, u2, m1, mlp_w1, w2, 1, final_g, None, T_LAT)[0]
    return out.reshape(BATCH, SEQ, D_MODEL)
```

```python
import functools
import math

import jax
import jax.numpy as jnp
from jax import lax
from jax.experimental import pallas as pl
from jax.experimental.pallas import tpu as pltpu

F32 = jnp.float32
BF16 = jnp.bfloat16

D_MODEL = 2048
BATCH = 8
SEQ = 2048
CTX_LEN = 256
GRID_W = 64
HEAD_DIM = 128
N_HEADS = 16
N_KV_HEADS = 4
GROUP = 4
WINDOW = 128
ROPE_BASE = 10000.0
D_FF = 4 * D_MODEL
N_MOD = 6
EPS = 1e-6
NEG = -1e30
LOG2E = math.log2(math.e)

T_LAT = BATCH * SEQ
T_CTX = BATCH * CTX_LEN
T_ALL = T_LAT + T_CTX
MOD_ROWS = 16
QKV_W = (N_HEADS + 2 * N_KV_HEADS) * HEAD_DIM
KWIN = 3 * WINDOW

VMEM_LIMIT = 56 * 1024 * 1024


def _params(sem):
    return pltpu.CompilerParams(dimension_semantics=sem, vmem_limit_bytes=VMEM_LIMIT)


def _mod_row(i, tm):
    return jnp.minimum((i * tm) // SEQ, BATCH)


def _mod_kernel(c_ref, w_ref, b_ref, o_ref):
    s = jax.nn.silu(c_ref[...]).astype(BF16)
    o_ref[...] = jnp.dot(s, w_ref[...].astype(BF16), preferred_element_type=F32) + b_ref[...]


def _mod_table(cvec, mod_w, mod_b):
    depth = mod_w.shape[0]
    n = N_MOD * D_MODEL
    tn = 1024
    out = pl.pallas_call(
        _mod_kernel,
        grid=(depth, n // tn),
        in_specs=[
            pl.BlockSpec((MOD_ROWS, D_MODEL), lambda l, j: (0, 0)),
            pl.BlockSpec((None, D_MODEL, tn), lambda l, j: (l, 0, j)),
            pl.BlockSpec((None, 1, tn), lambda l, j: (l, 0, j)),
        ],
        out_specs=pl.BlockSpec((None, MOD_ROWS, tn), lambda l, j: (l, 0, j)),
        out_shape=jax.ShapeDtypeStruct((depth, MOD_ROWS, n), F32),
        compiler_params=_params(("parallel", "parallel")),
        name="mod_table",
    )(cvec, mod_w, mod_b.reshape(depth, 1, n))
    return out.reshape(depth, MOD_ROWS, 1, n)


def _rms_mod(x, g_ref, sc_ref, sh_ref):
    y = x * lax.rsqrt(jnp.mean(x * x, axis=-1, keepdims=True) + EPS)
    y = y * g_ref[...]
    return y * (1.0 + sc_ref[...]) + sh_ref[...]


def _pick_stream(refs, tile, n_lat_tiles, fn):
    if len(refs) == 1:
        fn(refs[0])
        return
    pl.when(tile < n_lat_tiles)(lambda: fn(refs[0]))
    pl.when(tile >= n_lat_tiles)(lambda: fn(refs[1]))


def _stream_specs(n_src, tm, tile_of):
    n_lat = T_LAT // tm
    if n_src == 1:
        return [pl.BlockSpec((tm, D_MODEL), lambda *g: (tile_of(*g), 0))]
    return [
        pl.BlockSpec((tm, D_MODEL), lambda *g: (jnp.minimum(tile_of(*g), n_lat - 1), 0)),
        pl.BlockSpec((tm, D_MODEL), lambda *g: (jnp.maximum(tile_of(*g) - n_lat, 0), 0)),
    ]


def _norm_mod_kernel(*refs, n_lat_tiles):
    srcs = refs[:-4]
    g_ref, sc_ref, sh_ref, o_ref = refs[-4:]

    def emit(x_ref):
        o_ref[...] = _rms_mod(x_ref[...], g_ref, sc_ref, sh_ref).astype(o_ref.dtype)

    _pick_stream(srcs, pl.program_id(0), n_lat_tiles, emit)


def _norm_mod(srcs, g, mod, shift_blk, scale_blk, rows):
    tm = 512
    return pl.pallas_call(
        functools.partial(_norm_mod_kernel, n_lat_tiles=T_LAT // tm),
        grid=(rows // tm,),
        in_specs=_stream_specs(len(srcs), tm, lambda i: i) + [
            pl.BlockSpec((1, D_MODEL), lambda i: (0, 0)),
            pl.BlockSpec((None, 1, D_MODEL), lambda i: (_mod_row(i, tm), 0, scale_blk)),
            pl.BlockSpec((None, 1, D_MODEL), lambda i: (_mod_row(i, tm), 0, shift_blk)),
        ],
        out_specs=pl.BlockSpec((tm, D_MODEL), lambda i: (i, 0)),
        out_shape=jax.ShapeDtypeStruct((rows, D_MODEL), BF16),
        compiler_params=_params(("parallel",)),
        name="norm_mod",
    )(*srcs, g.reshape(1, D_MODEL), mod, mod)


def _ws_kernel(*refs, nw, ne, no, row_chunks, epilogue):
    x_ref = refs[0]
    w_refs = refs[1:1 + nw]
    e_refs = refs[1 + nw:1 + nw + ne]
    o_refs = refs[1 + nw + ne:1 + nw + ne + no]
    strip_refs = refs[1 + nw + ne + no:]

    @pl.when(pl.program_id(1) == 0)
    def _():
        for w_ref, strip in zip(w_refs, strip_refs):
            strip[...] = w_ref[...].astype(BF16)

    rc = x_ref.shape[0] // row_chunks
    for r in range(row_chunks):
        rows = pl.ds(r * rc, rc)
        x = x_ref[rows, :]
        accs = [jnp.dot(x, strip[...], preferred_element_type=F32) for strip in strip_refs]
        for o_ref, val in zip(o_refs, epilogue(accs, e_refs, rows)):
            o_ref[rows, :] = val.astype(o_ref.dtype)


def _matmul_ws(x, ws, extras, outs, epilogue, *, rows, tm, tn, n_cols, row_blk_off=0, row_chunks=1, name):
    kdim = x.shape[1]
    in_specs = [pl.BlockSpec((tm, kdim), lambda j, i: (i + row_blk_off, 0))]
    args = [x]
    for w, layer, off in ws:
        in_specs.append(pl.BlockSpec((None, kdim, tn), lambda j, i, layer=layer, off=off: (layer, 0, j + off)))
        args.append(w)
    for arr, blk, imap in extras:
        in_specs.append(pl.BlockSpec(blk, lambda j, i, imap=imap: imap(i, j)))
        args.append(arr)
    body = functools.partial(_ws_kernel, nw=len(ws), ne=len(extras), no=len(outs),
                             row_chunks=row_chunks, epilogue=epilogue)
    return pl.pallas_call(
        body,
        grid=(n_cols // tn, rows // tm),
        in_specs=in_specs,
        out_specs=[pl.BlockSpec((tm, tn), lambda j, i: (i, j)) for _ in outs],
        out_shape=[jax.ShapeDtypeStruct((rows, cols), dt) for cols, dt in outs],
        scratch_shapes=[pltpu.VMEM((kdim, tn), BF16) for _ in ws],
        compiler_params=_params(("arbitrary", "arbitrary")),
        name=name,
    )(*args)


def _mlp_down_kernel(*refs, nk, final, row_chunks):
    if final:
        x_ref, w_ref, h_ref, gate_ref, g_ref, out_ref, acc_ref = refs
        sc_ref = sh_ref = h_out = None
    else:
        x_ref, w_ref, h_ref, gate_ref, g_ref, sc_ref, sh_ref, h_out, out_ref, acc_ref = refs
    k = pl.program_id(1)
    j = pl.program_id(2)
    _, tm, tn = acc_ref.shape
    rc = tm // row_chunks

    @pl.when(k == 0)
    def _():
        acc_ref[j] = jnp.dot(x_ref[...], w_ref[...], preferred_element_type=F32)

    @pl.when(jnp.logical_and(k > 0, k < nk - 1))
    def _():
        acc_ref[j] += jnp.dot(x_ref[...], w_ref[...], preferred_element_type=F32)

    def h_new_rows(half, rows):
        acc = acc_ref[half, rows, :] + jnp.dot(x_ref[rows, :], w_ref[...], preferred_element_type=F32)
        h_new = h_ref[rows, :] + gate_ref[...] * acc
        if not final:
            h_out[rows, :] = h_new
        return h_new

    @pl.when(jnp.logical_and(k == nk - 1, j == 0))
    def _():
        for r in range(row_chunks):
            rows = pl.ds(r * rc, rc)
            acc_ref[0, rows, :] = h_new_rows(0, rows)

    @pl.when(jnp.logical_and(k == nk - 1, j == 1))
    def _():
        for r in range(row_chunks):
            rows = pl.ds(r * rc, rc)
            halves = (acc_ref[0, rows, :], h_new_rows(1, rows))
            ssq = sum(jnp.sum(hv * hv, axis=-1, keepdims=True) for hv in halves)
            rstd = lax.rsqrt(ssq * (1.0 / D_MODEL) + EPS)
            for half, hv in enumerate(halves):
                cols = slice(half * tn, (half + 1) * tn)
                y = hv * rstd * g_ref[:, cols]
                if not final:
                    y = y * (1.0 + sc_ref[:, cols]) + sh_ref[:, cols]
                out_ref[rows, cols] = y.astype(out_ref.dtype)


def _mlp_down(x, w, layer, h, mod, g, mod_next, *, rows):
    tm, tk = 1024, 2048
    tn = D_MODEL // 2
    nk = D_FF // tk
    final = mod_next is None

    def half_of(k, j):
        return jnp.where(k == nk - 1, j, 0)

    def mod_spec(blk, width):
        return pl.BlockSpec((None, 1, width), lambda i, k, j: (_mod_row(i, tm), 0, blk))

    in_specs = [
        pl.BlockSpec((tm, tk), lambda i, k, j: (i, k)),
        pl.BlockSpec((None, tk, tn), lambda i, k, j: (layer, k, j)),
        pl.BlockSpec((tm, tn), lambda i, k, j: (i, half_of(k, j))),
        pl.BlockSpec((None, 1, tn), lambda i, k, j: (_mod_row(i, tm), 0, 5 * 2 + half_of(k, j))),
        pl.BlockSpec((1, D_MODEL), lambda i, k, j: (0, 0)),
    ]
    args = [x, w, h, mod, g.reshape(1, D_MODEL)]
    full_rows = pl.BlockSpec((tm, D_MODEL), lambda i, k, j: (i, 0))
    if final:
        out_specs = [full_rows]
        out_shape = [jax.ShapeDtypeStruct((rows, D_MODEL), F32)]
    else:
        in_specs += [mod_spec(1, D_MODEL), mod_spec(0, D_MODEL)]
        args += [mod_next, mod_next]
        out_specs = [pl.BlockSpec((tm, tn), lambda i, k, j: (i, half_of(k, j))), full_rows]
        out_shape = [jax.ShapeDtypeStruct((rows, D_MODEL), F32), jax.ShapeDtypeStruct((rows, D_MODEL), BF16)]
    return pl.pallas_call(
        functools.partial(_mlp_down_kernel, nk=nk, final=final, row_chunks=4),
        grid=(rows // tm, nk, 2),
        in_specs=in_specs,
        out_specs=out_specs,
        out_shape=out_shape,
        scratch_shapes=[pltpu.VMEM((2, tm, tn), F32)],
        compiler_params=_params(("arbitrary", "arbitrary", "arbitrary")),
        name="mlp_down",
    )(*args)


def _epi_relu2(accs, e_refs, rows):
    a = jnp.maximum(accs[0], 0.0)
    return [a * a]


def _epi_plain(accs, e_refs, rows):
    return [accs[0]]


def _epi_rope(accs, e_refs, rows):
    cos_ref, sin_lo_ref, sin_hi_ref = e_refs
    acc = accs[0]
    cos, sin_lo, sin_hi = cos_ref[rows, :], sin_lo_ref[rows, :], sin_hi_ref[rows, :]
    heads = []
    for hh in range(acc.shape[1] // HEAD_DIM):
        a = acc[:, hh * HEAD_DIM:(hh + 1) * HEAD_DIM]
        up = pltpu.roll(a, HEAD_DIM - HEAD_DIM // 4, 1)
        dn = pltpu.roll(a, HEAD_DIM // 4, 1)
        heads.append(a * cos + up * sin_lo + dn * sin_hi)
    return [jnp.concatenate(heads, axis=1)]


def _proj_norm_kernel(*refs, n_src, n_cast, n_lat_tiles, row_chunks):
    x_ref, w_ref = refs[:2]
    h_refs = refs[2:2 + n_src]
    gate_ref, g_ref, sc_ref, sh_ref, h_out, u_out, wb_ref = refs[2 + n_src:]
    step = pl.program_id(0)
    kc = w_ref.shape[0]
    rc = x_ref.shape[0] // row_chunks

    @pl.when(step < n_cast)
    def _():
        wb_ref[pl.ds(pl.multiple_of(step * kc, kc), kc), :] = w_ref[...].astype(BF16)

    def compute(h_ref):
        for r in range(row_chunks):
            rows = pl.ds(r * rc, rc)
            acc = jnp.dot(x_ref[rows, :], wb_ref[...], preferred_element_type=F32)
            h_new = h_ref[rows, :] + gate_ref[...] * acc
            h_out[rows, :] = h_new
            u_out[rows, :] = _rms_mod(h_new, g_ref, sc_ref, sh_ref).astype(u_out.dtype)

    @pl.when(step >= n_cast)
    def _():
        _pick_stream(h_refs, step - n_cast, n_lat_tiles, compute)


def _proj_norm(x, w, layer, h_srcs, mod, g, rows):
    tm = 512
    n_cast = 4
    kc = D_MODEL // n_cast

    def tile(s):
        return jnp.maximum(s - n_cast, 0)

    def mod_spec(blk):
        return pl.BlockSpec((None, 1, D_MODEL), lambda s: (_mod_row(tile(s), tm), 0, blk))

    body = functools.partial(_proj_norm_kernel, n_src=len(h_srcs), n_cast=n_cast,
                             n_lat_tiles=T_LAT // tm, row_chunks=2)
    return pl.pallas_call(
        body,
        grid=(n_cast + rows // tm,),
        in_specs=[
            pl.BlockSpec((tm, D_MODEL), lambda s: (tile(s), 0)),
            pl.BlockSpec((None, kc, D_MODEL), lambda s: (layer, jnp.minimum(s, n_cast - 1), 0)),
        ] + _stream_specs(len(h_srcs), tm, tile) + [
            mod_spec(2),
            pl.BlockSpec((1, D_MODEL), lambda s: (0, 0)),
            mod_spec(4),
            mod_spec(3),
        ],
        out_specs=[pl.BlockSpec((tm, D_MODEL), lambda s: (tile(s), 0)),
                   pl.BlockSpec((tm, D_MODEL), lambda s: (tile(s), 0))],
        out_shape=[jax.ShapeDtypeStruct((rows, D_MODEL), F32),
                   jax.ShapeDtypeStruct((rows, D_MODEL), BF16)],
        scratch_shapes=[pltpu.VMEM((D_MODEL, D_MODEL), BF16)],
        compiler_params=_params(("arbitrary",)),
        name="proj_norm",
    )(x, w, *h_srcs, mod, g.reshape(1, D_MODEL), mod, mod)


CONV_PAD = 8


def _conv_in_kernel(x_ref, wb_ref, wc_ref, wh_ref, cw_ref, o_ref, sb_ref, sc_ref, sh_ref, zk_ref, bk_ref,
                    *, n_tiles, row_chunks):
    i = pl.program_id(1)
    tm, tn = o_ref.shape
    rc = tm // row_chunks

    @pl.when(i == 0)
    def _():
        sb_ref[...] = wb_ref[...].astype(BF16)
        sc_ref[...] = wc_ref[...].astype(BF16)
        sh_ref[...] = wh_ref[...].astype(BF16)
        zk_ref[...] = jnp.zeros(zk_ref.shape, F32)
        bk_ref[1] = jnp.zeros((tm, tn), F32)

    def finish_chunk(slot, r, z_first):
        row0 = (i - 1) * tm
        seq_len = jnp.where(row0 >= T_LAT, CTX_LEN, SEQ)
        base = r * rc
        row = lax.broadcasted_iota(jnp.int32, (rc, 1), 0)
        pos = (row0 + base + row) & (seq_len - 1)
        z_prev = zk_ref[slot, pl.ds(CONV_PAD - 1 + base, rc), :]
        z_mid = zk_ref[slot, pl.ds(CONV_PAD + base, rc), :]
        z_next = zk_ref[slot, pl.ds(CONV_PAD + 1 + base, rc), :]
        if r == row_chunks - 1:
            z_next = jnp.where(row == rc - 1, z_first, z_next)
        z_prev = jnp.where(pos == 0, 0.0, z_prev)
        z_next = jnp.where(pos == seq_len - 1, 0.0, z_next)
        conv = (z_prev * cw_ref[pl.ds(0, 1), :] + z_mid * cw_ref[pl.ds(1, 1), :]
                + z_next * cw_ref[pl.ds(2, 1), :])
        o_ref[pl.ds(base, rc), :] = (bk_ref[slot, pl.ds(base, rc), :] * conv).astype(o_ref.dtype)

    def project_tile(slot):
        z_first = None
        for r in range(row_chunks):
            rows = pl.ds(r * rc, rc)
            x = x_ref[rows, :]
            bk_ref[slot, rows, :] = jnp.dot(x, sb_ref[...], preferred_element_type=F32)
            z = (jnp.dot(x, sc_ref[...], preferred_element_type=F32)
                 * jnp.dot(x, sh_ref[...], preferred_element_type=F32))
            zk_ref[slot, pl.ds(CONV_PAD + r * rc, rc), :] = z
            if r == 0:
                z_first = z[0:1, :]
            finish_chunk(1 - slot, r, z_first)
        zk_ref[slot, pl.ds(CONV_PAD - 1, 1), :] = zk_ref[1 - slot, pl.ds(CONV_PAD + tm - 1, 1), :]

    for slot in range(2):
        pl.when(jnp.logical_and(i < n_tiles, i % 2 == slot))(functools.partial(project_tile, slot))

    @pl.when(i == n_tiles)
    def _():
        for r in range(row_chunks):
            finish_chunk((n_tiles - 1) % 2, r, jnp.zeros((1, tn), F32))


def _conv_in(u, w_in, conv_w, rows):
    tm, tn = 1024, 512
    n_tiles = rows // tm
    nblk = D_MODEL // tn

    def w_spec(part):
        return pl.BlockSpec((None, D_MODEL, tn), lambda j, i: (0, 0, j + part * nblk))

    return pl.pallas_call(
        functools.partial(_conv_in_kernel, n_tiles=n_tiles, row_chunks=8),
        grid=(nblk, n_tiles + 1),
        in_specs=[
            pl.BlockSpec((tm, D_MODEL), lambda j, i: (jnp.minimum(i, n_tiles - 1), 0)),
            w_spec(0), w_spec(1), w_spec(2),
            pl.BlockSpec((None, 3, tn), lambda j, i: (0, 0, j)),
        ],
        out_specs=pl.BlockSpec((tm, tn), lambda j, i: (jnp.maximum(i - 1, 0), j)),
        out_shape=jax.ShapeDtypeStruct((rows, D_MODEL), BF16),
        scratch_shapes=[pltpu.VMEM((D_MODEL, tn), BF16) for _ in range(3)] + [
            pltpu.VMEM((2, tm + 2 * CONV_PAD, tn), F32),
            pltpu.VMEM((2, tm, tn), F32),
        ],
        compiler_params=_params(("arbitrary", "arbitrary")),
        name="conv_in",
    )(u, w_in, w_in, w_in, conv_w)


def _attn_kernel(sink_ref, q_ref, k_ref, v_ref, kc_ref, vc_ref, o_ref, va_ref, vca_ref):
    kvh = pl.program_id(1)
    va_ref[:, :HEAD_DIM] = v_ref[...]
    va_ref[:, HEAD_DIM:] = jnp.ones((SEQ, HEAD_DIM), BF16)
    vca_ref[:, :HEAD_DIM] = vc_ref[...]
    vca_ref[:, HEAD_DIM:] = jnp.ones((CTX_LEN, HEAD_DIM), BF16)
    kc = kc_ref[...]
    vca = vca_ref[...]
    rel = (lax.broadcasted_iota(jnp.int32, (WINDOW, KWIN), 0)
           - lax.broadcasted_iota(jnp.int32, (WINDOW, KWIN), 1))
    nt = (((1,), (1,)), ((), ()))

    def body(n, carry):
        q0 = pl.multiple_of(n * WINDOW, WINDOW)
        start = pl.multiple_of(jnp.clip((n - 1) * WINDOW, 0, SEQ - KWIN), WINDOW)
        valid = jnp.abs(rel + (q0 - start)) <= WINDOW
        qs = jnp.concatenate(
            [q_ref[pl.ds(q0, WINDOW), g * HEAD_DIM:(g + 1) * HEAD_DIM] for g in range(GROUP)], axis=0)
        kw = k_ref[pl.ds(start, KWIN), :]
        vwa = va_ref[pl.ds(start, KWIN), :]
        s_loc = lax.dot_general(qs, kw, nt, preferred_element_type=F32)
        s_ctx = lax.dot_general(qs, kc, nt, preferred_element_type=F32)
        p_loc, p_ctx, sink_terms = [], [], []
        for g in range(GROUP):
            sl = jnp.where(valid, s_loc[g * WINDOW:(g + 1) * WINDOW], NEG)
            sc = s_ctx[g * WINDOW:(g + 1) * WINDOW]
            sk = sink_ref[kvh * GROUP + g] * LOG2E
            m = jnp.maximum(jnp.maximum(jnp.max(sl, axis=-1, keepdims=True),
                                        jnp.max(sc, axis=-1, keepdims=True)), sk)
            p_loc.append(jnp.exp2(sl - m).astype(BF16))
            p_ctx.append(jnp.exp2(sc - m).astype(BF16))
            sink_terms.append(jnp.exp2(sk - m))
        oa = (jnp.dot(jnp.concatenate(p_loc, axis=0), vwa, preferred_element_type=F32)
              + jnp.dot(jnp.concatenate(p_ctx, axis=0), vca, preferred_element_type=F32))
        o = oa[:, :HEAD_DIM] / (oa[:, HEAD_DIM:] + jnp.concatenate(sink_terms, axis=0))
        for g in range(GROUP):
            o_ref[pl.ds(q0, WINDOW), g * HEAD_DIM:(g + 1) * HEAD_DIM] = (
                o[g * WINDOW:(g + 1) * WINDOW].astype(o_ref.dtype))
        return carry

    lax.fori_loop(0, SEQ // WINDOW, body, 0, unroll=8)


def _attention(qkv, kvc, sink):
    qw = GROUP * HEAD_DIM
    k_off = N_HEADS
    v_off = N_HEADS + N_KV_HEADS
    return pl.pallas_call(
        _attn_kernel,
        grid=(BATCH, N_KV_HEADS),
        in_specs=[
            pl.BlockSpec(memory_space=pltpu.SMEM),
            pl.BlockSpec((SEQ, qw), lambda b, h: (b, h)),
            pl.BlockSpec((SEQ, HEAD_DIM), lambda b, h: (b, k_off + h)),
            pl.BlockSpec((SEQ, HEAD_DIM), lambda b, h: (b, v_off + h)),
            pl.BlockSpec((CTX_LEN, HEAD_DIM), lambda b, h: (b, h)),
            pl.BlockSpec((CTX_LEN, HEAD_DIM), lambda b, h: (b, N_KV_HEADS + h)),
        ],
        out_specs=pl.BlockSpec((SEQ, qw), lambda b, h: (b, h)),
        out_shape=jax.ShapeDtypeStruct((T_LAT, N_HEADS * HEAD_DIM), BF16),
        scratch_shapes=[pltpu.VMEM((SEQ, 2 * HEAD_DIM), BF16), pltpu.VMEM((CTX_LEN, 2 * HEAD_DIM), BF16)],
        compiler_params=_params(("parallel", "parallel")),
        name="window_attention",
    )(sink, qkv, qkv, qkv, kvc, kvc)


def _rope_tables():
    rows_n = SEQ // GRID_W
    row = jnp.repeat(jnp.arange(rows_n), GRID_W).astype(F32)
    col = jnp.tile(jnp.arange(GRID_W), rows_n).astype(F32)
    nf = HEAD_DIM // 4
    inv_freq = ROPE_BASE ** (-jnp.arange(nf, dtype=F32) / nf)
    ang_r = row[:, None] * inv_freq[None, :]
    ang_c = col[:, None] * inv_freq[None, :]
    ang = jnp.concatenate([ang_r, ang_r, ang_c, ang_c], axis=-1)
    cos, sin = jnp.cos(ang), jnp.sin(ang)
    low_half = (jnp.arange(HEAD_DIM) % (2 * nf)) < nf
    sin_lo = jnp.where(low_half[None, :], -sin, 0.0)
    sin_hi = jnp.where(low_half[None, :], 0.0, sin)
    scale = LOG2E / math.sqrt(HEAD_DIM)
    ones, zeros = jnp.ones_like(cos), jnp.zeros_like(cos)
    cos3 = jnp.stack([cos * scale, cos, ones])
    lo3 = jnp.stack([sin_lo * scale, sin_lo, zeros])
    hi3 = jnp.stack([sin_hi * scale, sin_hi, zeros])
    return cos3, lo3, hi3


def _mlp(h, u2, mod, w1, w2_bf16, layer, g_next, mod_next, rows):
    a = _matmul_ws(u2, [(w1, layer, 0)], [], [(D_FF, BF16)], _epi_relu2,
                   rows=rows, tm=2048, tn=1024, n_cols=D_FF, name="mlp_up")[0]
    return _mlp_down(a, w2_bf16, layer, h, mod, g_next, mod_next, rows=rows)


def kernel(x, c, ctx, c_ctx, norm1_g, norm2_g, mod_w, mod_b, conv_w_in, conv_w, conv_w_out,
           attn_w_qkv, attn_sink, attn_w_o, mlp_w1, mlp_w2, final_g):
    x_lat = x.reshape(T_LAT, D_MODEL)
    x_ctx = ctx.reshape(T_CTX, D_MODEL)
    cvec = jnp.concatenate(
        [c, c_ctx[None, :], jnp.zeros((MOD_ROWS - BATCH - 1, D_MODEL), F32)], axis=0)
    mod = _mod_table(cvec, mod_w, mod_b)
    w2 = mlp_w2.astype(BF16)

    m0 = mod[0]
    u = _norm_mod([x_lat, x_ctx], norm1_g[0], m0, 0, 1, T_ALL)
    v = _conv_in(u, conv_w_in, conv_w, T_ALL)
    h, u2 = _proj_norm(v, conv_w_out, 0, [x_lat, x_ctx], m0, norm2_g[0], T_ALL)
    m1 = mod[1]
    h, u = _mlp(h, u2, m0, mlp_w1, w2, 0, norm1_g[1], m1, T_ALL)

    w_qkv = attn_w_qkv
    cos3, lo3, hi3 = _rope_tables()
    tm, tn = 1024, 512
    pos_blocks = SEQ // tm
    qcols = N_HEADS * HEAD_DIM // tn

    def table_map(i, j):
        return (jnp.clip(j - (qcols - 1), 0, 2), i % pos_blocks, 0)

    rope_extras = [(t, (None, tm, HEAD_DIM), table_map) for t in (cos3, lo3, hi3)]
    qkv = _matmul_ws(u, [(w_qkv, 0, 0)], rope_extras, [(QKV_W, BF16)], _epi_rope,
                     rows=T_LAT, tm=tm, tn=tn, n_cols=QKV_W, row_chunks=4, name="qkv_latent")[0]
    kv_cols = 2 * N_KV_HEADS * HEAD_DIM
    kvc = _matmul_ws(u, [(w_qkv, 0, N_HEADS * HEAD_DIM // 512)], [], [(kv_cols, BF16)], _epi_plain,
                     rows=T_CTX, tm=1024, tn=512, n_cols=kv_cols,
                     row_blk_off=T_LAT // 1024, name="kv_context")[0]
    o = _attention(qkv, kvc, attn_sink[0])
    h, u2 = _proj_norm(o, attn_w_o, 0, [h], m1, norm2_g[1], T_LAT)
    out = _mlp(h, u2, m1, mlp_w1, w2, 1, final_g, None, T_LAT)[0]
    return out.reshape(BATCH, SEQ, D_MODEL)
```

```python
import functools
import math

import jax
import jax.numpy as jnp
from jax import lax
from jax.experimental import pallas as pl
from jax.experimental.pallas import tpu as pltpu

F32 = jnp.float32
BF16 = jnp.bfloat16

D_MODEL = 2048
BATCH = 8
SEQ = 2048
CTX_LEN = 256
GRID_W = 64
HEAD_DIM = 128
N_HEADS = 16
N_KV_HEADS = 4
GROUP = 4
WINDOW = 128
ROPE_BASE = 10000.0
D_FF = 4 * D_MODEL
N_MOD = 6
EPS = 1e-6
NEG = -1e30
LOG2E = math.log2(math.e)

T_LAT = BATCH * SEQ
T_CTX = BATCH * CTX_LEN
T_ALL = T_LAT + T_CTX
MOD_ROWS = 16
QKV_W = (N_HEADS + 2 * N_KV_HEADS) * HEAD_DIM
KWIN = 3 * WINDOW

VMEM_LIMIT = 56 * 1024 * 1024


def _params(sem):
    return pltpu.CompilerParams(dimension_semantics=sem, vmem_limit_bytes=VMEM_LIMIT)


def _mod_row(i, tm):
    return jnp.minimum((i * tm) // SEQ, BATCH)


def _mod_kernel(c_ref, w_ref, b_ref, o_ref):
    s = jax.nn.silu(c_ref[...]).astype(BF16)
    o_ref[...] = jnp.dot(s, w_ref[...].astype(BF16), preferred_element_type=F32) + b_ref[...]


def _mod_table(cvec, mod_w, mod_b):
    depth = mod_w.shape[0]
    n = N_MOD * D_MODEL
    tn = 1024
    out = pl.pallas_call(
        _mod_kernel,
        grid=(depth, n // tn),
        in_specs=[
            pl.BlockSpec((MOD_ROWS, D_MODEL), lambda l, j: (0, 0)),
            pl.BlockSpec((None, D_MODEL, tn), lambda l, j: (l, 0, j)),
            pl.BlockSpec((None, 1, tn), lambda l, j: (l, 0, j)),
        ],
        out_specs=pl.BlockSpec((None, MOD_ROWS, tn), lambda l, j: (l, 0, j)),
        out_shape=jax.ShapeDtypeStruct((depth, MOD_ROWS, n), F32),
        compiler_params=_params(("parallel", "parallel")),
        name="mod_table",
    )(cvec, mod_w, mod_b.reshape(depth, 1, n))
    return out.reshape(depth, MOD_ROWS, 1, n)


def _rms_mod(x, g_ref, sc_ref, sh_ref):
    y = x * lax.rsqrt(jnp.mean(x * x, axis=-1, keepdims=True) + EPS)
    y = y * g_ref[...]
    return y * (1.0 + sc_ref[...]) + sh_ref[...]


def _pick_stream(refs, tile, n_lat_tiles, fn):
    if len(refs) == 1:
        fn(refs[0])
        return
    pl.when(tile < n_lat_tiles)(lambda: fn(refs[0]))
    pl.when(tile >= n_lat_tiles)(lambda: fn(refs[1]))


def _stream_specs(n_src, tm, tile_of):
    n_lat = T_LAT // tm
    if n_src == 1:
        return [pl.BlockSpec((tm, D_MODEL), lambda *g: (tile_of(*g), 0))]
    return [
        pl.BlockSpec((tm, D_MODEL), lambda *g: (jnp.minimum(tile_of(*g), n_lat - 1), 0)),
        pl.BlockSpec((tm, D_MODEL), lambda *g: (jnp.maximum(tile_of(*g) - n_lat, 0), 0)),
    ]


def _norm_mod_kernel(*refs, n_lat_tiles):
    srcs = refs[:-4]
    g_ref, sc_ref, sh_ref, o_ref = refs[-4:]

    def emit(x_ref):
        rc = 64
        for r in range(o_ref.shape[0] // rc):
            rows = pl.ds(r * rc, rc)
            o_ref[rows, :] = _rms_mod(x_ref[rows, :], g_ref, sc_ref, sh_ref).astype(o_ref.dtype)

    _pick_stream(srcs, pl.program_id(0), n_lat_tiles, emit)


def _norm_mod(srcs, g, mod, shift_blk, scale_blk, rows):
    tm = 1024
    return pl.pallas_call(
        functools.partial(_norm_mod_kernel, n_lat_tiles=T_LAT // tm),
        grid=(rows // tm,),
        in_specs=_stream_specs(len(srcs), tm, lambda i: i) + [
            pl.BlockSpec((1, D_MODEL), lambda i: (0, 0)),
            pl.BlockSpec((None, 1, D_MODEL), lambda i: (_mod_row(i, tm), 0, scale_blk)),
            pl.BlockSpec((None, 1, D_MODEL), lambda i: (_mod_row(i, tm), 0, shift_blk)),
        ],
        out_specs=pl.BlockSpec((tm, D_MODEL), lambda i: (i, 0)),
        out_shape=jax.ShapeDtypeStruct((rows, D_MODEL), BF16),
        compiler_params=_params(("parallel",)),
        name="norm_mod",
    )(*srcs, g.reshape(1, D_MODEL), mod, mod)


def _ws_kernel(*refs, nw, ne, no, row_chunks, epilogue):
    x_ref = refs[0]
    w_refs = refs[1:1 + nw]
    e_refs = refs[1 + nw:1 + nw + ne]
    o_refs = refs[1 + nw + ne:1 + nw + ne + no]
    strip_refs = refs[1 + nw + ne + no:]

    @pl.when(pl.program_id(1) == 0)
    def _():
        for w_ref, strip in zip(w_refs, strip_refs):
            strip[...] = w_ref[...].astype(BF16)

    rc = x_ref.shape[0] // row_chunks
    for r in range(row_chunks):
        rows = pl.ds(r * rc, rc)
        x = x_ref[rows, :]
        accs = [jnp.dot(x, strip[...], preferred_element_type=F32) for strip in strip_refs]
        for o_ref, val in zip(o_refs, epilogue(accs, e_refs, rows)):
            o_ref[rows, :] = val.astype(o_ref.dtype)


def _matmul_ws(x, ws, extras, outs, epilogue, *, rows, tm, tn, n_cols, row_blk_off=0, row_chunks=1, name):
    kdim = x.shape[1]
    in_specs = [pl.BlockSpec((tm, kdim), lambda j, i: (i + row_blk_off, 0))]
    args = [x]
    for w, layer, off in ws:
        in_specs.append(pl.BlockSpec((None, kdim, tn), lambda j, i, layer=layer, off=off: (layer, 0, j + off)))
        args.append(w)
    for arr, blk, imap in extras:
        in_specs.append(pl.BlockSpec(blk, lambda j, i, imap=imap: imap(i, j)))
        args.append(arr)
    body = functools.partial(_ws_kernel, nw=len(ws), ne=len(extras), no=len(outs),
                             row_chunks=row_chunks, epilogue=epilogue)
    return pl.pallas_call(
        body,
        grid=(n_cols // tn, rows // tm),
        in_specs=in_specs,
        out_specs=[pl.BlockSpec((tm, tn), lambda j, i: (i, j)) for _ in outs],
        out_shape=[jax.ShapeDtypeStruct((rows, cols), dt) for cols, dt in outs],
        scratch_shapes=[pltpu.VMEM((kdim, tn), BF16) for _ in ws],
        compiler_params=_params(("arbitrary", "arbitrary")),
        name=name,
    )(*args)


def _mlp_down_kernel(*refs, nk, final, row_chunks):
    if final:
        x_ref, w_ref, h_ref, gate_ref, g_ref, out_ref, acc_ref = refs
        sc_ref = sh_ref = h_out = None
    else:
        x_ref, w_ref, h_ref, gate_ref, g_ref, sc_ref, sh_ref, h_out, out_ref, acc_ref = refs
    k = pl.program_id(1)
    j = pl.program_id(2)
    _, tm, tn = acc_ref.shape
    rc = tm // row_chunks

    @pl.when(k == 0)
    def _():
        acc_ref[j] = jnp.dot(x_ref[...], w_ref[...], preferred_element_type=F32)

    @pl.when(jnp.logical_and(k > 0, k < nk - 1))
    def _():
        acc_ref[j] += jnp.dot(x_ref[...], w_ref[...], preferred_element_type=F32)

    def h_new_rows(half, rows):
        acc = acc_ref[half, rows, :] + jnp.dot(x_ref[rows, :], w_ref[...], preferred_element_type=F32)
        h_new = h_ref[rows, :] + gate_ref[...] * acc
        if not final:
            h_out[rows, :] = h_new
        return h_new

    @pl.when(jnp.logical_and(k == nk - 1, j == 0))
    def _():
        for r in range(row_chunks):
            rows = pl.ds(r * rc, rc)
            acc_ref[0, rows, :] = h_new_rows(0, rows)

    @pl.when(jnp.logical_and(k == nk - 1, j == 1))
    def _():
        for r in range(row_chunks):
            rows = pl.ds(r * rc, rc)
            halves = (acc_ref[0, rows, :], h_new_rows(1, rows))
            ssq = sum(jnp.sum(hv * hv, axis=-1, keepdims=True) for hv in halves)
            rstd = lax.rsqrt(ssq * (1.0 / D_MODEL) + EPS)
            for half, hv in enumerate(halves):
                cols = slice(half * tn, (half + 1) * tn)
                y = hv * rstd * g_ref[:, cols]
                if not final:
                    y = y * (1.0 + sc_ref[:, cols]) + sh_ref[:, cols]
                out_ref[rows, cols] = y.astype(out_ref.dtype)


def _mlp_down(x, w, layer, h, mod, g, mod_next, *, rows):
    tm, tk = 1024, 2048
    tn = D_MODEL // 2
    nk = D_FF // tk
    final = mod_next is None

    def half_of(k, j):
        return jnp.where(k == nk - 1, j, 0)

    def mod_spec(blk, width):
        return pl.BlockSpec((None, 1, width), lambda i, k, j: (_mod_row(i, tm), 0, blk))

    in_specs = [
        pl.BlockSpec((tm, tk), lambda i, k, j: (i, k)),
        pl.BlockSpec((None, tk, tn), lambda i, k, j: (layer, k, j)),
        pl.BlockSpec((tm, tn), lambda i, k, j: (i, half_of(k, j))),
        pl.BlockSpec((None, 1, tn), lambda i, k, j: (_mod_row(i, tm), 0, 5 * 2 + half_of(k, j))),
        pl.BlockSpec((1, D_MODEL), lambda i, k, j: (0, 0)),
    ]
    args = [x, w, h, mod, g.reshape(1, D_MODEL)]
    full_rows = pl.BlockSpec((tm, D_MODEL), lambda i, k, j: (i, 0))
    if final:
        out_specs = [full_rows]
        out_shape = [jax.ShapeDtypeStruct((rows, D_MODEL), F32)]
    else:
        in_specs += [mod_spec(1, D_MODEL), mod_spec(0, D_MODEL)]
        args += [mod_next, mod_next]
        out_specs = [pl.BlockSpec((tm, tn), lambda i, k, j: (i, half_of(k, j))), full_rows]
        out_shape = [jax.ShapeDtypeStruct((rows, D_MODEL), F32), jax.ShapeDtypeStruct((rows, D_MODEL), BF16)]
    return pl.pallas_call(
        functools.partial(_mlp_down_kernel, nk=nk, final=final, row_chunks=4),
        grid=(rows // tm, nk, 2),
        in_specs=in_specs,
        out_specs=out_specs,
        out_shape=out_shape,
        scratch_shapes=[pltpu.VMEM((2, tm, tn), F32)],
        compiler_params=_params(("arbitrary", "arbitrary", "arbitrary")),
        name="mlp_down",
    )(*args)


def _epi_gate(accs, e_refs, rows):
    b_gate, c_gate, hval = accs
    return [b_gate, c_gate * hval]


def _epi_relu2(accs, e_refs, rows):
    a = jnp.maximum(accs[0], 0.0)
    return [a * a]


def _epi_plain(accs, e_refs, rows):
    return [accs[0]]


def _rope(a, cos, sin_lo, sin_hi):
    up = pltpu.roll(a, HEAD_DIM - HEAD_DIM // 4, 1)
    dn = pltpu.roll(a, HEAD_DIM // 4, 1)
    return a * cos + up * sin_lo + dn * sin_hi


def _qkv_kernel(x_ref, w_ref, cq_ref, lq_ref, hq_ref, ck_ref, lk_ref, hk_ref, o_ref, wb_ref,
                *, n_cast, row_chunks):
    step = pl.program_id(0)
    kc = w_ref.shape[0]
    rc = x_ref.shape[0] // row_chunks

    @pl.when(step < n_cast)
    def _():
        wb_ref[pl.ds(pl.multiple_of(step * kc, kc), kc), :] = w_ref[...].astype(BF16)

    @pl.when(step >= n_cast)
    def _():
        for r in range(row_chunks):
            rows = pl.ds(r * rc, rc)
            acc = jnp.dot(x_ref[rows, :], wb_ref[...], preferred_element_type=F32)
            q_tabs = (cq_ref[rows, :], lq_ref[rows, :], hq_ref[rows, :])
            k_tabs = (ck_ref[rows, :], lk_ref[rows, :], hk_ref[rows, :])
            for hh in range(N_HEADS + N_KV_HEADS):
                cols = slice(hh * HEAD_DIM, (hh + 1) * HEAD_DIM)
                tabs = q_tabs if hh < N_HEADS else k_tabs
                o_ref[rows, cols] = _rope(acc[:, cols], *tabs).astype(o_ref.dtype)
            v_cols = slice((N_HEADS + N_KV_HEADS) * HEAD_DIM, QKV_W)
            o_ref[rows, v_cols] = acc[:, v_cols].astype(o_ref.dtype)


def _qkv_latent(u, w_qkv, tables):
    tm = 512
    n_cast = 4
    kc = D_MODEL // n_cast
    pos_blocks = SEQ // tm

    def tile(s):
        t = jnp.maximum(s - n_cast, 0)
        return (t % BATCH) * pos_blocks + t // BATCH

    def table_spec(group):
        return pl.BlockSpec((None, tm, HEAD_DIM), lambda s: (group, jnp.maximum(s - n_cast, 0) // BATCH, 0))

    cos2, lo2, hi2 = tables
    return pl.pallas_call(
        functools.partial(_qkv_kernel, n_cast=n_cast, row_chunks=2),
        grid=(n_cast + T_LAT // tm,),
        in_specs=[
            pl.BlockSpec((tm, D_MODEL), lambda s: (tile(s), 0)),
            pl.BlockSpec((None, kc, QKV_W), lambda s: (0, jnp.minimum(s, n_cast - 1), 0)),
            table_spec(0), table_spec(0), table_spec(0),
            table_spec(1), table_spec(1), table_spec(1),
        ],
        out_specs=pl.BlockSpec((tm, QKV_W), lambda s: (tile(s), 0)),
        out_shape=jax.ShapeDtypeStruct((T_LAT, QKV_W), BF16),
        scratch_shapes=[pltpu.VMEM((D_MODEL, QKV_W), BF16)],
        compiler_params=_params(("arbitrary",)),
        name="qkv_latent",
    )(u, w_qkv, cos2, lo2, hi2, cos2, lo2, hi2)


def _proj_norm_kernel(*refs, n_src, n_cast, n_lat_tiles, row_chunks):
    x_ref, w_ref = refs[:2]
    h_refs = refs[2:2 + n_src]
    gate_ref, g_ref, sc_ref, sh_ref, h_out, u_out, wb_ref = refs[2 + n_src:]
    step = pl.program_id(0)
    kc = w_ref.shape[0]
    rc = x_ref.shape[0] // row_chunks

    @pl.when(step < n_cast)
    def _():
        wb_ref[pl.ds(pl.multiple_of(step * kc, kc), kc), :] = w_ref[...].astype(BF16)

    def compute(h_ref):
        for r in range(row_chunks):
            rows = pl.ds(r * rc, rc)
            acc = jnp.dot(x_ref[rows, :], wb_ref[...], preferred_element_type=F32)
            h_new = h_ref[rows, :] + gate_ref[...] * acc
            h_out[rows, :] = h_new
            u_out[rows, :] = _rms_mod(h_new, g_ref, sc_ref, sh_ref).astype(u_out.dtype)

    @pl.when(step >= n_cast)
    def _():
        _pick_stream(h_refs, step - n_cast, n_lat_tiles, compute)


def _proj_norm(x, w, layer, h_srcs, mod, g, rows):
    tm = 512
    n_cast = 4
    kc = D_MODEL // n_cast

    def tile(s):
        return jnp.maximum(s - n_cast, 0)

    def mod_spec(blk):
        return pl.BlockSpec((None, 1, D_MODEL), lambda s: (_mod_row(tile(s), tm), 0, blk))

    body = functools.partial(_proj_norm_kernel, n_src=len(h_srcs), n_cast=n_cast,
                             n_lat_tiles=T_LAT // tm, row_chunks=2)
    return pl.pallas_call(
        body,
        grid=(n_cast + rows // tm,),
        in_specs=[
            pl.BlockSpec((tm, D_MODEL), lambda s: (tile(s), 0)),
            pl.BlockSpec((None, kc, D_MODEL), lambda s: (layer, jnp.minimum(s, n_cast - 1), 0)),
        ] + _stream_specs(len(h_srcs), tm, tile) + [
            mod_spec(2),
            pl.BlockSpec((1, D_MODEL), lambda s: (0, 0)),
            mod_spec(4),
            mod_spec(3),
        ],
        out_specs=[pl.BlockSpec((tm, D_MODEL), lambda s: (tile(s), 0)),
                   pl.BlockSpec((tm, D_MODEL), lambda s: (tile(s), 0))],
        out_shape=[jax.ShapeDtypeStruct((rows, D_MODEL), F32),
                   jax.ShapeDtypeStruct((rows, D_MODEL), BF16)],
        scratch_shapes=[pltpu.VMEM((D_MODEL, D_MODEL), BF16)],
        compiler_params=_params(("arbitrary",)),
        name="proj_norm",
    )(x, w, *h_srcs, mod, g.reshape(1, D_MODEL), mod, mod)


HALO = 16


def _conv_gate_kernel(b_ref, z_ref, zp_ref, zn_ref, cw_ref, o_ref, pad_ref, *, tm):
    i = pl.program_id(0)
    row0 = i * tm
    seq_len = jnp.where(row0 >= T_LAT, CTX_LEN, SEQ)
    pos = (row0 + lax.broadcasted_iota(jnp.int32, (tm, 1), 0)) & (seq_len - 1)
    pad_ref[pl.ds(8, tm), :] = z_ref[...].astype(F32)
    pad_ref[pl.ds(7, 1), :] = zp_ref[pl.ds(HALO - 1, 1), :].astype(F32)
    pad_ref[pl.ds(8 + tm, 1), :] = zn_ref[pl.ds(0, 1), :].astype(F32)
    z_prev = jnp.where(pos == 0, 0.0, pad_ref[pl.ds(7, tm), :])
    z_next = jnp.where(pos == seq_len - 1, 0.0, pad_ref[pl.ds(9, tm), :])
    z_mid = pad_ref[pl.ds(8, tm), :]
    conv = z_prev * cw_ref[pl.ds(0, 1), :] + z_mid * cw_ref[pl.ds(1, 1), :] + z_next * cw_ref[pl.ds(2, 1), :]
    o_ref[...] = (b_ref[...].astype(F32) * conv).astype(o_ref.dtype)


def _conv_gate(b_gate, z, conv_w, rows):
    tm = 512
    per = tm // HALO
    last = rows // HALO - 1
    z3 = z.reshape(rows // HALO, HALO, D_MODEL)
    return pl.pallas_call(
        functools.partial(_conv_gate_kernel, tm=tm),
        grid=(rows // tm,),
        in_specs=[
            pl.BlockSpec((tm, D_MODEL), lambda i: (i, 0)),
            pl.BlockSpec((tm, D_MODEL), lambda i: (i, 0)),
            pl.BlockSpec((None, HALO, D_MODEL), lambda i: (jnp.maximum(i * per - 1, 0), 0, 0)),
            pl.BlockSpec((None, HALO, D_MODEL), lambda i: (jnp.minimum((i + 1) * per, last), 0, 0)),
            pl.BlockSpec((3, D_MODEL), lambda i: (0, 0)),
        ],
        out_specs=pl.BlockSpec((tm, D_MODEL), lambda i: (i, 0)),
        out_shape=jax.ShapeDtypeStruct((rows, D_MODEL), BF16),
        scratch_shapes=[pltpu.VMEM((tm + 16, D_MODEL), F32)],
        compiler_params=_params(("parallel",)),
        name="conv_gate",
    )(b_gate, z, z3, z3, conv_w)


def _attn_kernel(sink_ref, q_ref, k_ref, v_ref, kc_ref, vc_ref, o_ref, va_ref, vca_ref):
    kvh = pl.program_id(1)
    va_ref[:, :HEAD_DIM] = v_ref[...]
    va_ref[:, HEAD_DIM:] = jnp.ones((SEQ, HEAD_DIM), BF16)
    vca_ref[:, :HEAD_DIM] = vc_ref[...]
    vca_ref[:, HEAD_DIM:] = jnp.ones((CTX_LEN, HEAD_DIM), BF16)
    kc = kc_ref[...]
    vca = vca_ref[...]
    rel = (lax.broadcasted_iota(jnp.int32, (WINDOW, KWIN), 0)
           - lax.broadcasted_iota(jnp.int32, (WINDOW, KWIN), 1))
    nt = (((1,), (1,)), ((), ()))

    def body(n, carry):
        q0 = pl.multiple_of(n * WINDOW, WINDOW)
        start = pl.multiple_of(jnp.clip((n - 1) * WINDOW, 0, SEQ - KWIN), WINDOW)
        valid = jnp.abs(rel + (q0 - start)) <= WINDOW
        qs = jnp.concatenate(
            [q_ref[pl.ds(q0, WINDOW), g * HEAD_DIM:(g + 1) * HEAD_DIM] for g in range(GROUP)], axis=0)
        kw = k_ref[pl.ds(start, KWIN), :]
        vwa = va_ref[pl.ds(start, KWIN), :]
        s_loc = lax.dot_general(qs, kw, nt, preferred_element_type=F32)
        s_ctx = lax.dot_general(qs, kc, nt, preferred_element_type=F32)
        p_loc, p_ctx, sink_terms = [], [], []
        for g in range(GROUP):
            sl = jnp.where(valid, s_loc[g * WINDOW:(g + 1) * WINDOW], NEG)
            sc = s_ctx[g * WINDOW:(g + 1) * WINDOW]
            sk = sink_ref[kvh * GROUP + g] * LOG2E
            m = jnp.maximum(jnp.maximum(jnp.max(sl, axis=-1, keepdims=True),
                                        jnp.max(sc, axis=-1, keepdims=True)), sk)
            p_loc.append(jnp.exp2(sl - m).astype(BF16))
            p_ctx.append(jnp.exp2(sc - m).astype(BF16))
            sink_terms.append(jnp.exp2(sk - m))
        oa = (jnp.dot(jnp.concatenate(p_loc, axis=0), vwa, preferred_element_type=F32)
              + jnp.dot(jnp.concatenate(p_ctx, axis=0), vca, preferred_element_type=F32))
        o = oa[:, :HEAD_DIM] / (oa[:, HEAD_DIM:] + jnp.concatenate(sink_terms, axis=0))
        for g in range(GROUP):
            o_ref[pl.ds(q0, WINDOW), g * HEAD_DIM:(g + 1) * HEAD_DIM] = (
                o[g * WINDOW:(g + 1) * WINDOW].astype(o_ref.dtype))
        return carry

    lax.fori_loop(0, SEQ // WINDOW, body, 0, unroll=8)


def _attention(qkv, kvc, sink):
    qw = GROUP * HEAD_DIM
    k_off = N_HEADS
    v_off = N_HEADS + N_KV_HEADS
    return pl.pallas_call(
        _attn_kernel,
        grid=(BATCH, N_KV_HEADS),
        in_specs=[
            pl.BlockSpec(memory_space=pltpu.SMEM),
            pl.BlockSpec((SEQ, qw), lambda b, h: (b, h)),
            pl.BlockSpec((SEQ, HEAD_DIM), lambda b, h: (b, k_off + h)),
            pl.BlockSpec((SEQ, HEAD_DIM), lambda b, h: (b, v_off + h)),
            pl.BlockSpec((CTX_LEN, HEAD_DIM), lambda b, h: (b, h)),
            pl.BlockSpec((CTX_LEN, HEAD_DIM), lambda b, h: (b, N_KV_HEADS + h)),
        ],
        out_specs=pl.BlockSpec((SEQ, qw), lambda b, h: (b, h)),
        out_shape=jax.ShapeDtypeStruct((T_LAT, N_HEADS * HEAD_DIM), BF16),
        scratch_shapes=[pltpu.VMEM((SEQ, 2 * HEAD_DIM), BF16), pltpu.VMEM((CTX_LEN, 2 * HEAD_DIM), BF16)],
        compiler_params=_params(("parallel", "parallel")),
        name="window_attention",
    )(sink, qkv, qkv, qkv, kvc, kvc)


def _rope_tables():
    rows_n = SEQ // GRID_W
    row = jnp.repeat(jnp.arange(rows_n), GRID_W).astype(F32)
    col = jnp.tile(jnp.arange(GRID_W), rows_n).astype(F32)
    nf = HEAD_DIM // 4
    inv_freq = ROPE_BASE ** (-jnp.arange(nf, dtype=F32) / nf)
    ang_r = row[:, None] * inv_freq[None, :]
    ang_c = col[:, None] * inv_freq[None, :]
    ang = jnp.concatenate([ang_r, ang_r, ang_c, ang_c], axis=-1)
    cos, sin = jnp.cos(ang), jnp.sin(ang)
    low_half = (jnp.arange(HEAD_DIM) % (2 * nf)) < nf
    sin_lo = jnp.where(low_half[None, :], -sin, 0.0)
    sin_hi = jnp.where(low_half[None, :], 0.0, sin)
    scale = LOG2E / math.sqrt(HEAD_DIM)
    return (jnp.stack([cos * scale, cos]), jnp.stack([sin_lo * scale, sin_lo]),
            jnp.stack([sin_hi * scale, sin_hi]))


def _mlp(h, u2, mod, w1, w2_bf16, layer, g_next, mod_next, rows):
    a = _matmul_ws(u2, [(w1, layer, 0)], [], [(D_FF, BF16)], _epi_relu2,
                   rows=rows, tm=2048, tn=1024, n_cols=D_FF, name="mlp_up")[0]
    return _mlp_down(a, w2_bf16, layer, h, mod, g_next, mod_next, rows=rows)


def kernel(x, c, ctx, c_ctx, norm1_g, norm2_g, mod_w, mod_b, conv_w_in, conv_w, conv_w_out,
           attn_w_qkv, attn_sink, attn_w_o, mlp_w1, mlp_w2, final_g):
    x_lat = x.reshape(T_LAT, D_MODEL)
    x_ctx = ctx.reshape(T_CTX, D_MODEL)
    cvec = jnp.concatenate(
        [c, c_ctx[None, :], jnp.zeros((MOD_ROWS - BATCH - 1, D_MODEL), F32)], axis=0)
    mod = _mod_table(cvec, mod_w, mod_b)
    w2 = mlp_w2.astype(BF16)

    m0 = mod[0]
    u = _norm_mod([x_lat, x_ctx], norm1_g[0], m0, 0, 1, T_ALL)
    nblk = D_MODEL // 512
    b_gate, z = _matmul_ws(u, [(conv_w_in, 0, 0), (conv_w_in, 0, nblk), (conv_w_in, 0, 2 * nblk)], [],
                           [(D_MODEL, BF16), (D_MODEL, BF16)], _epi_gate,
                           rows=T_ALL, tm=1024, tn=512, n_cols=D_MODEL, name="conv_in_proj")
    v = _conv_gate(b_gate, z, conv_w[0], T_ALL)
    h, u2 = _proj_norm(v, conv_w_out, 0, [x_lat, x_ctx], m0, norm2_g[0], T_ALL)
    m1 = mod[1]
    h, u = _mlp(h, u2, m0, mlp_w1, w2, 0, norm1_g[1], m1, T_ALL)

    w_qkv = attn_w_qkv
    qkv = _qkv_latent(u, w_qkv, _rope_tables())
    kv_cols = 2 * N_KV_HEADS * HEAD_DIM
    kvc = _matmul_ws(u, [(w_qkv, 0, N_HEADS * HEAD_DIM // 512)], [], [(kv_cols, BF16)], _epi_plain,
                     rows=T_CTX, tm=1024, tn=512, n_cols=kv_cols,
                     row_blk_off=T_LAT // 1024, name="kv_context")[0]
    o = _attention(qkv, kvc, attn_sink[0])
    h, u2 = _proj_norm(o, attn_w_o, 0, [h], m1, norm2_g[1], T_LAT)
    out = _mlp(h, u2, m1, mlp_w1, w2, 1, final_g, None, T_LAT)[0]
    return out.reshape(BATCH, SEQ, D_MODEL)
```

```python
import functools
import math

import jax
import jax.numpy as jnp
from jax import lax
from jax.experimental import pallas as pl
from jax.experimental.pallas import tpu as pltpu

F32 = jnp.float32
BF16 = jnp.bfloat16

D_MODEL = 2048
BATCH = 8
SEQ = 2048
CTX_LEN = 256
GRID_W = 64
HEAD_DIM = 128
N_HEADS = 16
N_KV_HEADS = 4
GROUP = 4
WINDOW = 128
ROPE_BASE = 10000.0
D_FF = 4 * D_MODEL
N_MOD = 6
EPS = 1e-6
NEG = -1e30
LOG2E = math.log2(math.e)

T_LAT = BATCH * SEQ
T_CTX = BATCH * CTX_LEN
T_ALL = T_LAT + T_CTX
MOD_ROWS = 16
QKV_W = (N_HEADS + 2 * N_KV_HEADS) * HEAD_DIM
KWIN = 3 * WINDOW

VMEM_LIMIT = 60 * 1024 * 1024


def _params(sem):
    return pltpu.CompilerParams(dimension_semantics=sem, vmem_limit_bytes=VMEM_LIMIT)


def _mod_row(i, tm):
    return jnp.minimum((i * tm) // SEQ, BATCH)


def _mod_kernel(c_ref, w_ref, b_ref, o_ref):
    s = jax.nn.silu(c_ref[...]).astype(BF16)
    o_ref[...] = jnp.dot(s, w_ref[...].astype(BF16), preferred_element_type=F32) + b_ref[...]


def _mod_table(cvec, mod_w, mod_b):
    depth = mod_w.shape[0]
    n = N_MOD * D_MODEL
    tn = 1024
    out = pl.pallas_call(
        _mod_kernel,
        grid=(depth, n // tn),
        in_specs=[
            pl.BlockSpec((MOD_ROWS, D_MODEL), lambda l, j: (0, 0)),
            pl.BlockSpec((None, D_MODEL, tn), lambda l, j: (l, 0, j)),
            pl.BlockSpec((None, 1, tn), lambda l, j: (l, 0, j)),
        ],
        out_specs=pl.BlockSpec((None, MOD_ROWS, tn), lambda l, j: (l, 0, j)),
        out_shape=jax.ShapeDtypeStruct((depth, MOD_ROWS, n), F32),
        compiler_params=_params(("parallel", "parallel")),
        name="mod_table",
    )(cvec, mod_w, mod_b.reshape(depth, 1, n))
    return out.reshape(depth, MOD_ROWS, 1, n)


def _rms_mod(x, g_ref, sc_ref, sh_ref):
    y = x * lax.rsqrt(jnp.mean(x * x, axis=-1, keepdims=True) + EPS)
    y = y * g_ref[...]
    return y * (1.0 + sc_ref[...]) + sh_ref[...]


def _pick_stream(refs, tile, n_lat_tiles, fn):
    if len(refs) == 1:
        fn(refs[0])
        return
    pl.when(tile < n_lat_tiles)(lambda: fn(refs[0]))
    pl.when(tile >= n_lat_tiles)(lambda: fn(refs[1]))


def _stream_specs(n_src, tm, tile_of):
    n_lat = T_LAT // tm
    if n_src == 1:
        return [pl.BlockSpec((tm, D_MODEL), lambda *g: (tile_of(*g), 0))]
    return [
        pl.BlockSpec((tm, D_MODEL), lambda *g: (jnp.minimum(tile_of(*g), n_lat - 1), 0)),
        pl.BlockSpec((tm, D_MODEL), lambda *g: (jnp.maximum(tile_of(*g) - n_lat, 0), 0)),
    ]


def _norm_mod_kernel(*refs, n_lat_tiles):
    srcs = refs[:-4]
    g_ref, sc_ref, sh_ref, o_ref = refs[-4:]

    def emit(x_ref):
        rc = 64
        for r in range(o_ref.shape[0] // rc):
            rows = pl.ds(r * rc, rc)
            o_ref[rows, :] = _rms_mod(x_ref[rows, :], g_ref, sc_ref, sh_ref).astype(o_ref.dtype)

    _pick_stream(srcs, pl.program_id(0), n_lat_tiles, emit)


def _norm_mod(srcs, g, mod, shift_blk, scale_blk, rows):
    tm = 1024
    return pl.pallas_call(
        functools.partial(_norm_mod_kernel, n_lat_tiles=T_LAT // tm),
        grid=(rows // tm,),
        in_specs=_stream_specs(len(srcs), tm, lambda i: i) + [
            pl.BlockSpec((1, D_MODEL), lambda i: (0, 0)),
            pl.BlockSpec((None, 1, D_MODEL), lambda i: (_mod_row(i, tm), 0, scale_blk)),
            pl.BlockSpec((None, 1, D_MODEL), lambda i: (_mod_row(i, tm), 0, shift_blk)),
        ],
        out_specs=pl.BlockSpec((tm, D_MODEL), lambda i: (i, 0)),
        out_shape=jax.ShapeDtypeStruct((rows, D_MODEL), BF16),
        compiler_params=_params(("parallel",)),
        name="norm_mod",
    )(*srcs, g.reshape(1, D_MODEL), mod, mod)


def _ws_kernel(*refs, nw, ne, no, ns, row_chunks, epilogue):
    refs = list(refs)
    x_ref = refs.pop(0)
    w_refs = [refs.pop(0) for _ in range(nw)]
    e_refs = [refs.pop(0) for _ in range(ne)]
    side_in = [refs.pop(0) for _ in range(ns)]
    o_refs = [refs.pop(0) for _ in range(no)]
    side_out = [refs.pop(0) for _ in range(ns)]
    strip_refs = refs

    @pl.when(pl.program_id(1) == 0)
    def _():
        for w_ref, strip in zip(w_refs, strip_refs):
            strip[...] = w_ref[...].astype(BF16)

    for src, dst in zip(side_in, side_out):
        dst[...] = src[...].astype(BF16)

    rc = x_ref.shape[0] // row_chunks
    for r in range(row_chunks):
        rows = pl.ds(r * rc, rc)
        x = x_ref[rows, :]
        accs = [jnp.dot(x, strip[...], preferred_element_type=F32) for strip in strip_refs]
        for o_ref, val in zip(o_refs, epilogue(accs, e_refs, rows)):
            o_ref[rows, :] = val.astype(o_ref.dtype)


def _matmul_ws(x, ws, extras, outs, epilogue, *, rows, tm, tn, n_cols, row_blk_off=0, row_chunks=1,
               side=None, name):
    kdim = x.shape[1]
    n_j, n_i = n_cols // tn, rows // tm
    in_specs = [pl.BlockSpec((tm, kdim), lambda j, i: (i + row_blk_off, 0))]
    args = [x]
    for w, layer, off in ws:
        in_specs.append(pl.BlockSpec((None, kdim, tn), lambda j, i, layer=layer, off=off: (layer, 0, j + off)))
        args.append(w)
    for arr, blk, imap in extras:
        in_specs.append(pl.BlockSpec(blk, lambda j, i, imap=imap: imap(i, j)))
        args.append(arr)
    out_specs = [pl.BlockSpec((tm, tn), lambda j, i: (i, j)) for _ in outs]
    out_shape = [jax.ShapeDtypeStruct((rows, cols), dt) for cols, dt in outs]
    if side is not None:
        side_w, side_layer, per_j = side
        _, side_rows, side_cols = side_w.shape
        rb = side_rows // (n_j * per_j)

        def side_blk(j, i):
            return j * per_j + jnp.minimum(i, per_j - 1)

        in_specs.append(pl.BlockSpec((None, rb, side_cols), lambda j, i: (side_layer, side_blk(j, i), 0)))
        args.append(side_w)
        out_specs.append(pl.BlockSpec((rb, side_cols), lambda j, i: (side_blk(j, i), 0)))
        out_shape.append(jax.ShapeDtypeStruct((side_rows, side_cols), BF16))
    body = functools.partial(_ws_kernel, nw=len(ws), ne=len(extras), no=len(outs),
                             ns=0 if side is None else 1, row_chunks=row_chunks, epilogue=epilogue)
    return pl.pallas_call(
        body,
        grid=(n_j, n_i),
        in_specs=in_specs,
        out_specs=out_specs,
        out_shape=out_shape,
        scratch_shapes=[pltpu.VMEM((kdim, tn), BF16) for _ in ws],
        compiler_params=_params(("arbitrary", "arbitrary")),
        name=name,
    )(*args)


def _mlp_down_kernel(*refs, nk, final, row_chunks):
    if final:
        x_ref, w_ref, h_ref, gate_ref, g_ref, out_ref, acc_ref = refs
        sc_ref = sh_ref = h_out = None
    else:
        x_ref, w_ref, h_ref, gate_ref, g_ref, sc_ref, sh_ref, h_out, out_ref, acc_ref = refs
    k = pl.program_id(1)
    j = pl.program_id(2)
    _, tm, tn = acc_ref.shape
    rc = tm // row_chunks

    @pl.when(k == 0)
    def _():
        acc_ref[j] = jnp.dot(x_ref[...], w_ref[...], preferred_element_type=F32)

    @pl.when(jnp.logical_and(k > 0, k < nk - 1))
    def _():
        acc_ref[j] += jnp.dot(x_ref[...], w_ref[...], preferred_element_type=F32)

    def h_new_rows(half, rows):
        acc = acc_ref[half, rows, :] + jnp.dot(x_ref[rows, :], w_ref[...], preferred_element_type=F32)
        h_new = h_ref[rows, :] + gate_ref[...] * acc
        if not final:
            h_out[rows, :] = h_new
        return h_new

    @pl.when(jnp.logical_and(k == nk - 1, j == 0))
    def _():
        for r in range(row_chunks):
            rows = pl.ds(r * rc, rc)
            acc_ref[0, rows, :] = h_new_rows(0, rows)

    @pl.when(jnp.logical_and(k == nk - 1, j == 1))
    def _():
        for r in range(row_chunks):
            rows = pl.ds(r * rc, rc)
            halves = (acc_ref[0, rows, :], h_new_rows(1, rows))
            ssq = sum(jnp.sum(hv * hv, axis=-1, keepdims=True) for hv in halves)
            rstd = lax.rsqrt(ssq * (1.0 / D_MODEL) + EPS)
            for half, hv in enumerate(halves):
                cols = slice(half * tn, (half + 1) * tn)
                y = hv * rstd * g_ref[:, cols]
                if not final:
                    y = y * (1.0 + sc_ref[:, cols]) + sh_ref[:, cols]
                out_ref[rows, cols] = y.astype(out_ref.dtype)


def _mlp_down(x, w, h, mod, g, mod_next, *, rows):
    tm, tk = 1024, 2048
    tn = D_MODEL // 2
    nk = D_FF // tk
    final = mod_next is None

    def half_of(k, j):
        return jnp.where(k == nk - 1, j, 0)

    def mod_spec(blk, width):
        return pl.BlockSpec((None, 1, width), lambda i, k, j: (_mod_row(i, tm), 0, blk))

    in_specs = [
        pl.BlockSpec((tm, tk), lambda i, k, j: (i, k)),
        pl.BlockSpec((tk, tn), lambda i, k, j: (k, j)),
        pl.BlockSpec((tm, tn), lambda i, k, j: (i, half_of(k, j))),
        pl.BlockSpec((None, 1, tn), lambda i, k, j: (_mod_row(i, tm), 0, 5 * 2 + half_of(k, j))),
        pl.BlockSpec((1, D_MODEL), lambda i, k, j: (0, 0)),
    ]
    args = [x, w, h, mod, g.reshape(1, D_MODEL)]
    full_rows = pl.BlockSpec((tm, D_MODEL), lambda i, k, j: (i, 0))
    if final:
        out_specs = [full_rows]
        out_shape = [jax.ShapeDtypeStruct((rows, D_MODEL), F32)]
    else:
        in_specs += [mod_spec(1, D_MODEL), mod_spec(0, D_MODEL)]
        args += [mod_next, mod_next]
        out_specs = [pl.BlockSpec((tm, tn), lambda i, k, j: (i, half_of(k, j))), full_rows]
        out_shape = [jax.ShapeDtypeStruct((rows, D_MODEL), F32), jax.ShapeDtypeStruct((rows, D_MODEL), BF16)]
    return pl.pallas_call(
        functools.partial(_mlp_down_kernel, nk=nk, final=final, row_chunks=4),
        grid=(rows // tm, nk, 2),
        in_specs=in_specs,
        out_specs=out_specs,
        out_shape=out_shape,
        scratch_shapes=[pltpu.VMEM((2, tm, tn), F32)],
        compiler_params=_params(("arbitrary", "arbitrary", "arbitrary")),
        name="mlp_down",
    )(*args)


def _epi_gate(accs, e_refs, rows):
    b_gate, c_gate, hval = accs
    return [b_gate, c_gate * hval]


def _epi_relu2(accs, e_refs, rows):
    a = jnp.maximum(accs[0], 0.0)
    return [a * a]


def _epi_plain(accs, e_refs, rows):
    return [accs[0]]


def _rope(a, cos, sin_lo, sin_hi):
    up = pltpu.roll(a, HEAD_DIM - HEAD_DIM // 4, 1)
    dn = pltpu.roll(a, HEAD_DIM // 4, 1)
    return a * cos + up * sin_lo + dn * sin_hi


def _qkv_kernel(x_ref, w_ref, cq_ref, lq_ref, hq_ref, ck_ref, lk_ref, hk_ref, o_ref, wb_ref,
                *, n_cast, row_chunks):
    step = pl.program_id(0)
    kc = w_ref.shape[0]
    rc = x_ref.shape[0] // row_chunks

    @pl.when(step < n_cast)
    def _():
        wb_ref[pl.ds(pl.multiple_of(step * kc, kc), kc), :] = w_ref[...].astype(BF16)

    @pl.when(step >= n_cast)
    def _():
        for r in range(row_chunks):
            rows = pl.ds(r * rc, rc)
            acc = jnp.dot(x_ref[rows, :], wb_ref[...], preferred_element_type=F32)
            q_tabs = (cq_ref[rows, :], lq_ref[rows, :], hq_ref[rows, :])
            k_tabs = (ck_ref[rows, :], lk_ref[rows, :], hk_ref[rows, :])
            for hh in range(N_HEADS + N_KV_HEADS):
                cols = slice(hh * HEAD_DIM, (hh + 1) * HEAD_DIM)
                tabs = q_tabs if hh < N_HEADS else k_tabs
                o_ref[rows, cols] = _rope(acc[:, cols], *tabs).astype(o_ref.dtype)
            v_cols = slice((N_HEADS + N_KV_HEADS) * HEAD_DIM, QKV_W)
            o_ref[rows, v_cols] = acc[:, v_cols].astype(o_ref.dtype)


def _qkv_latent(u, w_qkv, tables):
    tm = 512
    n_cast = 4
    kc = D_MODEL // n_cast
    pos_blocks = SEQ // tm

    def tile(s):
        t = jnp.maximum(s - n_cast, 0)
        return (t % BATCH) * pos_blocks + t // BATCH

    def table_spec(group):
        return pl.BlockSpec((None, tm, HEAD_DIM), lambda s: (group, jnp.maximum(s - n_cast, 0) // BATCH, 0))

    cos2, lo2, hi2 = tables
    return pl.pallas_call(
        functools.partial(_qkv_kernel, n_cast=n_cast, row_chunks=2),
        grid=(n_cast + T_LAT // tm,),
        in_specs=[
            pl.BlockSpec((tm, D_MODEL), lambda s: (tile(s), 0)),
            pl.BlockSpec((None, kc, QKV_W), lambda s: (0, jnp.minimum(s, n_cast - 1), 0)),
            table_spec(0), table_spec(0), table_spec(0),
            table_spec(1), table_spec(1), table_spec(1),
        ],
        out_specs=pl.BlockSpec((tm, QKV_W), lambda s: (tile(s), 0)),
        out_shape=jax.ShapeDtypeStruct((T_LAT, QKV_W), BF16),
        scratch_shapes=[pltpu.VMEM((D_MODEL, QKV_W), BF16)],
        compiler_params=_params(("arbitrary",)),
        name="qkv_latent",
    )(u, w_qkv, cos2, lo2, hi2, cos2, lo2, hi2)


def _proj_norm_kernel(*refs, n_src, n_cast, n_lat_tiles, row_chunks):
    x_ref, w_ref = refs[:2]
    h_refs = refs[2:2 + n_src]
    gate_ref, g_ref, sc_ref, sh_ref, h_out, u_out, wb_ref = refs[2 + n_src:]
    step = pl.program_id(0)
    kc = w_ref.shape[0]
    rc = x_ref.shape[0] // row_chunks

    @pl.when(step < n_cast)
    def _():
        wb_ref[pl.ds(pl.multiple_of(step * kc, kc), kc), :] = w_ref[...].astype(BF16)

    def compute(h_ref):
        for r in range(row_chunks):
            rows = pl.ds(r * rc, rc)
            acc = jnp.dot(x_ref[rows, :], wb_ref[...], preferred_element_type=F32)
            h_new = h_ref[rows, :] + gate_ref[...] * acc
            h_out[rows, :] = h_new
            u_out[rows, :] = _rms_mod(h_new, g_ref, sc_ref, sh_ref).astype(u_out.dtype)

    @pl.when(step >= n_cast)
    def _():
        _pick_stream(h_refs, step - n_cast, n_lat_tiles, compute)


def _proj_norm(x, w, layer, h_srcs, mod, g, rows):
    tm = 512
    n_cast = 4
    kc = D_MODEL // n_cast

    def tile(s):
        return jnp.maximum(s - n_cast, 0)

    def mod_spec(blk):
        return pl.BlockSpec((None, 1, D_MODEL), lambda s: (_mod_row(tile(s), tm), 0, blk))

    body = functools.partial(_proj_norm_kernel, n_src=len(h_srcs), n_cast=n_cast,
                             n_lat_tiles=T_LAT // tm, row_chunks=2)
    return pl.pallas_call(
        body,
        grid=(n_cast + rows // tm,),
        in_specs=[
            pl.BlockSpec((tm, D_MODEL), lambda s: (tile(s), 0)),
            pl.BlockSpec((None, kc, D_MODEL), lambda s: (layer, jnp.minimum(s, n_cast - 1), 0)),
        ] + _stream_specs(len(h_srcs), tm, tile) + [
            mod_spec(2),
            pl.BlockSpec((1, D_MODEL), lambda s: (0, 0)),
            mod_spec(4),
            mod_spec(3),
        ],
        out_specs=[pl.BlockSpec((tm, D_MODEL), lambda s: (tile(s), 0)),
                   pl.BlockSpec((tm, D_MODEL), lambda s: (tile(s), 0))],
        out_shape=[jax.ShapeDtypeStruct((rows, D_MODEL), F32),
                   jax.ShapeDtypeStruct((rows, D_MODEL), BF16)],
        scratch_shapes=[pltpu.VMEM((D_MODEL, D_MODEL), BF16)],
        compiler_params=_params(("arbitrary",)),
        name="proj_norm",
    )(x, w, *h_srcs, mod, g.reshape(1, D_MODEL), mod, mod)


HALO = 16


def _conv_gate_kernel(b_ref, z_ref, zp_ref, zn_ref, cw_ref, o_ref, *, tm):
    i = pl.program_id(0)
    row0 = i * tm
    seq_len = jnp.where(row0 >= T_LAT, CTX_LEN, SEQ)
    starts_seq = (row0 & (seq_len - 1)) == 0
    ends_seq = ((row0 + tm) & (seq_len - 1)) == 0
    sub = 8
    cw = 512
    for c in range(D_MODEL // cw):
        cols = slice(c * cw, (c + 1) * cw)
        head = jnp.where(starts_seq, 0.0, zp_ref[:, cols].astype(F32)[HALO - sub:, :])
        tail = jnp.where(ends_seq, 0.0, zn_ref[:, cols].astype(F32)[:sub, :])
        zp = jnp.concatenate([head, z_ref[:, cols].astype(F32), tail], axis=0)
        z_prev = pltpu.roll(zp, 1, 0)[sub:sub + tm, :]
        z_next = pltpu.roll(zp, tm + 2 * sub - 1, 0)[sub:sub + tm, :]
        z_mid = zp[sub:sub + tm, :]
        conv = (z_prev * cw_ref[pl.ds(0, 1), cols] + z_mid * cw_ref[pl.ds(1, 1), cols]
                + z_next * cw_ref[pl.ds(2, 1), cols])
        o_ref[:, cols] = (b_ref[:, cols].astype(F32) * conv).astype(o_ref.dtype)


def _conv_gate(b_gate, z, conv_w, rows):
    tm = CTX_LEN
    assert SEQ % tm == 0
    per = tm // HALO
    last = rows // HALO - 1
    z3 = z.reshape(rows // HALO, HALO, D_MODEL)
    return pl.pallas_call(
        functools.partial(_conv_gate_kernel, tm=tm),
        grid=(rows // tm,),
        in_specs=[
            pl.BlockSpec((tm, D_MODEL), lambda i: (i, 0)),
            pl.BlockSpec((tm, D_MODEL), lambda i: (i, 0)),
            pl.BlockSpec((None, HALO, D_MODEL), lambda i: (jnp.maximum(i * per - 1, 0), 0, 0)),
            pl.BlockSpec((None, HALO, D_MODEL), lambda i: (jnp.minimum((i + 1) * per, last), 0, 0)),
            pl.BlockSpec((3, D_MODEL), lambda i: (0, 0)),
        ],
        out_specs=pl.BlockSpec((tm, D_MODEL), lambda i: (i, 0)),
        out_shape=jax.ShapeDtypeStruct((rows, D_MODEL), BF16),
        compiler_params=_params(("parallel",)),
        name="conv_gate",
    )(b_gate, z, z3, z3, conv_w)


def _attn_kernel(sink_ref, q_ref, k_ref, v_ref, kc_ref, vc_ref, o_ref, va_ref, vca_ref):
    kvh = pl.program_id(1)
    va_ref[:, :HEAD_DIM] = v_ref[...]
    va_ref[:, HEAD_DIM:] = jnp.ones((SEQ, HEAD_DIM), BF16)
    vca_ref[:, :HEAD_DIM] = vc_ref[...]
    vca_ref[:, HEAD_DIM:] = jnp.ones((CTX_LEN, HEAD_DIM), BF16)
    kc = kc_ref[...]
    vca = vca_ref[...]
    rel = (lax.broadcasted_iota(jnp.int32, (WINDOW, KWIN), 0)
           - lax.broadcasted_iota(jnp.int32, (WINDOW, KWIN), 1))
    nt = (((1,), (1,)), ((), ()))

    def body(n, carry):
        q0 = pl.multiple_of(n * WINDOW, WINDOW)
        start = pl.multiple_of(jnp.clip((n - 1) * WINDOW, 0, SEQ - KWIN), WINDOW)
        valid = jnp.abs(rel + (q0 - start)) <= WINDOW
        qs = jnp.concatenate(
            [q_ref[pl.ds(q0, WINDOW), g * HEAD_DIM:(g + 1) * HEAD_DIM] for g in range(GROUP)], axis=0)
        kw = k_ref[pl.ds(start, KWIN), :]
        vwa = va_ref[pl.ds(start, KWIN), :]
        s_loc = lax.dot_general(qs, kw, nt, preferred_element_type=F32)
        s_ctx = lax.dot_general(qs, kc, nt, preferred_element_type=F32)
        p_loc, p_ctx, sink_terms = [], [], []
        for g in range(GROUP):
            sl = jnp.where(valid, s_loc[g * WINDOW:(g + 1) * WINDOW], NEG)
            sc = s_ctx[g * WINDOW:(g + 1) * WINDOW]
            sk = sink_ref[kvh * GROUP + g] * LOG2E
            m = jnp.maximum(jnp.maximum(jnp.max(sl, axis=-1, keepdims=True),
                                        jnp.max(sc, axis=-1, keepdims=True)), sk)
            p_loc.append(jnp.exp2(sl - m).astype(BF16))
            p_ctx.append(jnp.exp2(sc - m).astype(BF16))
            sink_terms.append(jnp.exp2(sk - m))
        oa = (jnp.dot(jnp.concatenate(p_loc, axis=0), vwa, preferred_element_type=F32)
              + jnp.dot(jnp.concatenate(p_ctx, axis=0), vca, preferred_element_type=F32))
        o = oa[:, :HEAD_DIM] / (oa[:, HEAD_DIM:] + jnp.concatenate(sink_terms, axis=0))
        for g in range(GROUP):
            o_ref[pl.ds(q0, WINDOW), g * HEAD_DIM:(g + 1) * HEAD_DIM] = (
                o[g * WINDOW:(g + 1) * WINDOW].astype(o_ref.dtype))
        return carry

    lax.fori_loop(0, SEQ // WINDOW, body, 0, unroll=16)


def _attention(qkv, kvc, sink):
    qw = GROUP * HEAD_DIM
    k_off = N_HEADS
    v_off = N_HEADS + N_KV_HEADS
    return pl.pallas_call(
        _attn_kernel,
        grid=(BATCH, N_KV_HEADS),
        in_specs=[
            pl.BlockSpec(memory_space=pltpu.SMEM),
            pl.BlockSpec((SEQ, qw), lambda b, h: (b, h)),
            pl.BlockSpec((SEQ, HEAD_DIM), lambda b, h: (b, k_off + h)),
            pl.BlockSpec((SEQ, HEAD_DIM), lambda b, h: (b, v_off + h)),
            pl.BlockSpec((CTX_LEN, HEAD_DIM), lambda b, h: (b, h)),
            pl.BlockSpec((CTX_LEN, HEAD_DIM), lambda b, h: (b, N_KV_HEADS + h)),
        ],
        out_specs=pl.BlockSpec((SEQ, qw), lambda b, h: (b, h)),
        out_shape=jax.ShapeDtypeStruct((T_LAT, N_HEADS * HEAD_DIM), BF16),
        scratch_shapes=[pltpu.VMEM((SEQ, 2 * HEAD_DIM), BF16), pltpu.VMEM((CTX_LEN, 2 * HEAD_DIM), BF16)],
        compiler_params=_params(("parallel", "parallel")),
        name="window_attention",
    )(sink, qkv, qkv, qkv, kvc, kvc)


def _rope_tables():
    rows_n = SEQ // GRID_W
    row = jnp.repeat(jnp.arange(rows_n), GRID_W).astype(F32)
    col = jnp.tile(jnp.arange(GRID_W), rows_n).astype(F32)
    nf = HEAD_DIM // 4
    inv_freq = ROPE_BASE ** (-jnp.arange(nf, dtype=F32) / nf)
    ang_r = row[:, None] * inv_freq[None, :]
    ang_c = col[:, None] * inv_freq[None, :]
    ang = jnp.concatenate([ang_r, ang_r, ang_c, ang_c], axis=-1)
    cos, sin = jnp.cos(ang), jnp.sin(ang)
    low_half = (jnp.arange(HEAD_DIM) % (2 * nf)) < nf
    sin_lo = jnp.where(low_half[None, :], -sin, 0.0)
    sin_hi = jnp.where(low_half[None, :], 0.0, sin)
    scale = LOG2E / math.sqrt(HEAD_DIM)
    return (jnp.stack([cos * scale, cos]), jnp.stack([sin_lo * scale, sin_lo]),
            jnp.stack([sin_hi * scale, sin_hi]))


def _mlp(h, u2, mod, w1, w2, layer, g_next, mod_next, rows):
    tm = 2048
    a, w2_bf16 = _matmul_ws(u2, [(w1, layer, 0)], [], [(D_FF, BF16)], _epi_relu2,
                            rows=rows, tm=tm, tn=1024, n_cols=D_FF, side=(w2, layer, T_LAT // tm),
                            name="mlp_up")
    return _mlp_down(a, w2_bf16, h, mod, g_next, mod_next, rows=rows)


def kernel(x, c, ctx, c_ctx, norm1_g, norm2_g, mod_w, mod_b, conv_w_in, conv_w, conv_w_out,
           attn_w_qkv, attn_sink, attn_w_o, mlp_w1, mlp_w2, final_g):
    x_lat = x.reshape(T_LAT, D_MODEL)
    x_ctx = ctx.reshape(T_CTX, D_MODEL)
    cvec = jnp.concatenate(
        [c, c_ctx[None, :], jnp.zeros((MOD_ROWS - BATCH - 1, D_MODEL), F32)], axis=0)
    mod = _mod_table(cvec, mod_w, mod_b)

    m0 = mod[0]
    u = _norm_mod([x_lat, x_ctx], norm1_g[0], m0, 0, 1, T_ALL)
    nblk = D_MODEL // 512
    b_gate, z = _matmul_ws(u, [(conv_w_in, 0, 0), (conv_w_in, 0, nblk), (conv_w_in, 0, 2 * nblk)], [],
                           [(D_MODEL, BF16), (D_MODEL, BF16)], _epi_gate,
                           rows=T_ALL, tm=1024, tn=512, n_cols=D_MODEL, name="conv_in_proj")
    v = _conv_gate(b_gate, z, conv_w[0], T_ALL)
    h, u2 = _proj_norm(v, conv_w_out, 0, [x_lat, x_ctx], m0, norm2_g[0], T_ALL)
    m1 = mod[1]
    h, u = _mlp(h, u2, m0, mlp_w1, mlp_w2, 0, norm1_g[1], m1, T_ALL)

    w_qkv = attn_w_qkv
    qkv = _qkv_latent(u, w_qkv, _rope_tables())
    kv_cols = 2 * N_KV_HEADS * HEAD_DIM
    kvc = _matmul_ws(u, [(w_qkv, 0, N_HEADS * HEAD_DIM // 512)], [], [(kv_cols, BF16)], _epi_plain,
                     rows=T_CTX, tm=1024, tn=512, n_cols=kv_cols,
                     row_blk_off=T_LAT // 1024, name="kv_context")[0]
    o = _attention(qkv, kvc, attn_sink[0])
    h, u2 = _proj_norm(o, attn_w_o, 0, [h], m1, norm2_g[1], T_LAT)
    out = _mlp(h, u2, m1, mlp_w1, mlp_w2, 1, final_g, None, T_LAT)[0]
    return out.reshape(BATCH, SEQ, D_MODEL)
```

```python
import functools
import math

import jax
import jax.numpy as jnp
from jax import lax
from jax.experimental import pallas as pl
from jax.experimental.pallas import tpu as pltpu

F32 = jnp.float32
BF16 = jnp.bfloat16

D_MODEL = 2048
BATCH = 8
SEQ = 2048
CTX_LEN = 256
GRID_W = 64
HEAD_DIM = 128
N_HEADS = 16
N_KV_HEADS = 4
GROUP = 4
WINDOW = 128
ROPE_BASE = 10000.0
D_FF = 4 * D_MODEL
N_MOD = 6
EPS = 1e-6
NEG = -1e30
LOG2E = math.log2(math.e)

T_LAT = BATCH * SEQ
T_CTX = BATCH * CTX_LEN
T_ALL = T_LAT + T_CTX
MOD_ROWS = 16
QKV_W = (N_HEADS + 2 * N_KV_HEADS) * HEAD_DIM
KWIN = 3 * WINDOW

VMEM_LIMIT = 60 * 1024 * 1024


def _params(sem):
    return pltpu.CompilerParams(dimension_semantics=sem, vmem_limit_bytes=VMEM_LIMIT)


def _mod_row(i, tm):
    return jnp.minimum((i * tm) // SEQ, BATCH)


def _mod_kernel(c_ref, w_ref, b_ref, o_ref):
    s = jax.nn.silu(c_ref[...]).astype(BF16)
    o_ref[...] = jnp.dot(s, w_ref[...].astype(BF16), preferred_element_type=F32) + b_ref[...]


def _mod_table(cvec, mod_w, mod_b):
    depth = mod_w.shape[0]
    n = N_MOD * D_MODEL
    tn = 1024
    out = pl.pallas_call(
        _mod_kernel,
        grid=(depth, n // tn),
        in_specs=[
            pl.BlockSpec((MOD_ROWS, D_MODEL), lambda l, j: (0, 0)),
            pl.BlockSpec((None, D_MODEL, tn), lambda l, j: (l, 0, j)),
            pl.BlockSpec((None, 1, tn), lambda l, j: (l, 0, j)),
        ],
        out_specs=pl.BlockSpec((None, MOD_ROWS, tn), lambda l, j: (l, 0, j)),
        out_shape=jax.ShapeDtypeStruct((depth, MOD_ROWS, n), F32),
        compiler_params=_params(("parallel", "parallel")),
        name="mod_table",
    )(cvec, mod_w, mod_b.reshape(depth, 1, n))
    return out.reshape(depth, MOD_ROWS, 1, n)


def _rms_mod(x, g_ref, sc_ref, sh_ref):
    y = x * lax.rsqrt(jnp.mean(x * x, axis=-1, keepdims=True) + EPS)
    y = y * g_ref[...]
    return y * (1.0 + sc_ref[...]) + sh_ref[...]


def _pick_stream(refs, tile, n_lat_tiles, fn):
    if len(refs) == 1:
        fn(refs[0])
        return
    pl.when(tile < n_lat_tiles)(lambda: fn(refs[0]))
    pl.when(tile >= n_lat_tiles)(lambda: fn(refs[1]))


def _stream_specs(n_src, tm, tile_of):
    n_lat = T_LAT // tm
    if n_src == 1:
        return [pl.BlockSpec((tm, D_MODEL), lambda *g: (tile_of(*g), 0))]
    return [
        pl.BlockSpec((tm, D_MODEL), lambda *g: (jnp.minimum(tile_of(*g), n_lat - 1), 0)),
        pl.BlockSpec((tm, D_MODEL), lambda *g: (jnp.maximum(tile_of(*g) - n_lat, 0), 0)),
    ]


def _norm_mod_kernel(*refs, n_lat_tiles):
    srcs = refs[:-4]
    g_ref, sc_ref, sh_ref, o_ref = refs[-4:]

    def emit(x_ref):
        rc = 64
        for r in range(o_ref.shape[0] // rc):
            rows = pl.ds(r * rc, rc)
            o_ref[rows, :] = _rms_mod(x_ref[rows, :], g_ref, sc_ref, sh_ref).astype(o_ref.dtype)

    _pick_stream(srcs, pl.program_id(0), n_lat_tiles, emit)


def _norm_mod(srcs, g, mod, shift_blk, scale_blk, rows):
    tm = 1024
    return pl.pallas_call(
        functools.partial(_norm_mod_kernel, n_lat_tiles=T_LAT // tm),
        grid=(rows // tm,),
        in_specs=_stream_specs(len(srcs), tm, lambda i: i) + [
            pl.BlockSpec((1, D_MODEL), lambda i: (0, 0)),
            pl.BlockSpec((None, 1, D_MODEL), lambda i: (_mod_row(i, tm), 0, scale_blk)),
            pl.BlockSpec((None, 1, D_MODEL), lambda i: (_mod_row(i, tm), 0, shift_blk)),
        ],
        out_specs=pl.BlockSpec((tm, D_MODEL), lambda i: (i, 0)),
        out_shape=jax.ShapeDtypeStruct((rows, D_MODEL), BF16),
        compiler_params=_params(("parallel",)),
        name="norm_mod",
    )(*srcs, g.reshape(1, D_MODEL), mod, mod)


def _ws_kernel(*refs, nw, ne, no, ns, row_chunks, epilogue):
    refs = list(refs)
    x_ref = refs.pop(0)
    w_refs = [refs.pop(0) for _ in range(nw)]
    e_refs = [refs.pop(0) for _ in range(ne)]
    side_in = [refs.pop(0) for _ in range(ns)]
    o_refs = [refs.pop(0) for _ in range(no)]
    side_out = [refs.pop(0) for _ in range(ns)]
    strip_refs = refs

    @pl.when(pl.program_id(1) == 0)
    def _():
        for w_ref, strip in zip(w_refs, strip_refs):
            strip[...] = w_ref[...].astype(BF16)

    for src, dst in zip(side_in, side_out):
        dst[...] = src[...].astype(BF16)

    rc = x_ref.shape[0] // row_chunks
    for r in range(row_chunks):
        rows = pl.ds(r * rc, rc)
        x = x_ref[rows, :]
        accs = [jnp.dot(x, strip[...], preferred_element_type=F32) for strip in strip_refs]
        for o_ref, val in zip(o_refs, epilogue(accs, e_refs, rows)):
            o_ref[rows, :] = val.astype(o_ref.dtype)


def _matmul_ws(x, ws, extras, outs, epilogue, *, rows, tm, tn, n_cols, row_blk_off=0, row_chunks=1,
               side=None, name):
    kdim = x.shape[1]
    n_j, n_i = n_cols // tn, rows // tm
    in_specs = [pl.BlockSpec((tm, kdim), lambda j, i: (i + row_blk_off, 0))]
    args = [x]
    for w, layer, off in ws:
        in_specs.append(pl.BlockSpec((None, kdim, tn), lambda j, i, layer=layer, off=off: (layer, 0, j + off)))
        args.append(w)
    for arr, blk, imap in extras:
        in_specs.append(pl.BlockSpec(blk, lambda j, i, imap=imap: imap(i, j)))
        args.append(arr)
    out_specs = [pl.BlockSpec((tm, tn), lambda j, i: (i, j)) for _ in outs]
    out_shape = [jax.ShapeDtypeStruct((rows, cols), dt) for cols, dt in outs]
    if side is not None:
        side_w, side_layer, per_j = side
        _, side_rows, side_cols = side_w.shape
        rb = side_rows // (n_j * per_j)

        def side_blk(j, i):
            return j * per_j + jnp.minimum(i, per_j - 1)

        in_specs.append(pl.BlockSpec((None, rb, side_cols), lambda j, i: (side_layer, side_blk(j, i), 0)))
        args.append(side_w)
        out_specs.append(pl.BlockSpec((rb, side_cols), lambda j, i: (side_blk(j, i), 0)))
        out_shape.append(jax.ShapeDtypeStruct((side_rows, side_cols), BF16))
    body = functools.partial(_ws_kernel, nw=len(ws), ne=len(extras), no=len(outs),
                             ns=0 if side is None else 1, row_chunks=row_chunks, epilogue=epilogue)
    return pl.pallas_call(
        body,
        grid=(n_j, n_i),
        in_specs=in_specs,
        out_specs=out_specs,
        out_shape=out_shape,
        scratch_shapes=[pltpu.VMEM((kdim, tn), BF16) for _ in ws],
        compiler_params=_params(("arbitrary", "arbitrary")),
        name=name,
    )(*args)


def _mlp_down_kernel(*refs, nk, final, row_chunks):
    if final:
        x_ref, w_ref, h_ref, gate_ref, g_ref, out_ref, acc_ref = refs
        sc_ref = sh_ref = h_out = None
    else:
        x_ref, w_ref, h_ref, gate_ref, g_ref, sc_ref, sh_ref, h_out, out_ref, acc_ref = refs
    k = pl.program_id(1)
    j = pl.program_id(2)
    _, tm, tn = acc_ref.shape
    rc = tm // row_chunks

    @pl.when(k == 0)
    def _():
        acc_ref[j] = jnp.dot(x_ref[...], w_ref[...], preferred_element_type=F32)

    @pl.when(jnp.logical_and(k > 0, k < nk - 1))
    def _():
        acc_ref[j] += jnp.dot(x_ref[...], w_ref[...], preferred_element_type=F32)

    def h_new_rows(half, rows):
        acc = acc_ref[half, rows, :] + jnp.dot(x_ref[rows, :], w_ref[...], preferred_element_type=F32)
        h_new = h_ref[rows, :] + gate_ref[...] * acc
        if not final:
            h_out[rows, :] = h_new
        return h_new

    @pl.when(jnp.logical_and(k == nk - 1, j == 0))
    def _():
        for r in range(row_chunks):
            rows = pl.ds(r * rc, rc)
            acc_ref[0, rows, :] = h_new_rows(0, rows)

    @pl.when(jnp.logical_and(k == nk - 1, j == 1))
    def _():
        for r in range(row_chunks):
            rows = pl.ds(r * rc, rc)
            halves = (acc_ref[0, rows, :], h_new_rows(1, rows))
            ssq = sum(jnp.sum(hv * hv, axis=-1, keepdims=True) for hv in halves)
            rstd = lax.rsqrt(ssq * (1.0 / D_MODEL) + EPS)
            for half, hv in enumerate(halves):
                cols = slice(half * tn, (half + 1) * tn)
                y = hv * rstd * g_ref[:, cols]
                if not final:
                    y = y * (1.0 + sc_ref[:, cols]) + sh_ref[:, cols]
                out_ref[rows, cols] = y.astype(out_ref.dtype)


def _mlp_down(x, w, h, mod, g, mod_next, *, rows):
    tm, tk = 1024, 2048
    tn = D_MODEL // 2
    nk = D_FF // tk
    final = mod_next is None

    def half_of(k, j):
        return jnp.where(k == nk - 1, j, 0)

    def mod_spec(blk, width):
        return pl.BlockSpec((None, 1, width), lambda i, k, j: (_mod_row(i, tm), 0, blk))

    in_specs = [
        pl.BlockSpec((tm, tk), lambda i, k, j: (i, k)),
        pl.BlockSpec((tk, tn), lambda i, k, j: (k, j)),
        pl.BlockSpec((tm, tn), lambda i, k, j: (i, half_of(k, j))),
        pl.BlockSpec((None, 1, tn), lambda i, k, j: (_mod_row(i, tm), 0, 5 * 2 + half_of(k, j))),
        pl.BlockSpec((1, D_MODEL), lambda i, k, j: (0, 0)),
    ]
    args = [x, w, h, mod, g.reshape(1, D_MODEL)]
    full_rows = pl.BlockSpec((tm, D_MODEL), lambda i, k, j: (i, 0))
    if final:
        out_specs = [full_rows]
        out_shape = [jax.ShapeDtypeStruct((rows, D_MODEL), F32)]
    else:
        in_specs += [mod_spec(1, D_MODEL), mod_spec(0, D_MODEL)]
        args += [mod_next, mod_next]
        out_specs = [pl.BlockSpec((tm, tn), lambda i, k, j: (i, half_of(k, j))), full_rows]
        out_shape = [jax.ShapeDtypeStruct((rows, D_MODEL), F32), jax.ShapeDtypeStruct((rows, D_MODEL), BF16)]
    return pl.pallas_call(
        functools.partial(_mlp_down_kernel, nk=nk, final=final, row_chunks=4),
        grid=(rows // tm, nk, 2),
        in_specs=in_specs,
        out_specs=out_specs,
        out_shape=out_shape,
        scratch_shapes=[pltpu.VMEM((2, tm, tn), F32)],
        compiler_params=_params(("arbitrary", "arbitrary", "arbitrary")),
        name="mlp_down",
    )(*args)


def _epi_gate(accs, e_refs, rows):
    b_gate, c_gate, hval = accs
    return [b_gate, c_gate * hval]


def _epi_relu2(accs, e_refs, rows):
    a = jnp.maximum(accs[0], 0.0)
    return [a * a]


def _epi_plain(accs, e_refs, rows):
    return [accs[0]]


def _rope(a, cos, sin_lo, sin_hi):
    up = pltpu.roll(a, HEAD_DIM - HEAD_DIM // 4, 1)
    dn = pltpu.roll(a, HEAD_DIM // 4, 1)
    return a * cos + up * sin_lo + dn * sin_hi


def _qkv_kernel(x_ref, w_ref, cq_ref, lq_ref, hq_ref, ck_ref, lk_ref, hk_ref, o_ref, wb_ref,
                *, n_cast, row_chunks):
    step = pl.program_id(0)
    kc = w_ref.shape[0]
    rc = x_ref.shape[0] // row_chunks

    @pl.when(step < n_cast)
    def _():
        wb_ref[pl.ds(pl.multiple_of(step * kc, kc), kc), :] = w_ref[...].astype(BF16)

    @pl.when(step >= n_cast)
    def _():
        for r in range(row_chunks):
            rows = pl.ds(r * rc, rc)
            acc = jnp.dot(x_ref[rows, :], wb_ref[...], preferred_element_type=F32)
            q_tabs = (cq_ref[rows, :], lq_ref[rows, :], hq_ref[rows, :])
            k_tabs = (ck_ref[rows, :], lk_ref[rows, :], hk_ref[rows, :])
            for hh in range(N_HEADS + N_KV_HEADS):
                cols = slice(hh * HEAD_DIM, (hh + 1) * HEAD_DIM)
                tabs = q_tabs if hh < N_HEADS else k_tabs
                o_ref[rows, cols] = _rope(acc[:, cols], *tabs).astype(o_ref.dtype)
            v_cols = slice((N_HEADS + N_KV_HEADS) * HEAD_DIM, QKV_W)
            o_ref[rows, v_cols] = acc[:, v_cols].astype(o_ref.dtype)


def _qkv_latent(u, w_qkv, tables):
    tm = 512
    n_cast = 4
    kc = D_MODEL // n_cast
    pos_blocks = SEQ // tm

    def tile(s):
        t = jnp.maximum(s - n_cast, 0)
        return (t % BATCH) * pos_blocks + t // BATCH

    def table_spec(group):
        return pl.BlockSpec((None, tm, HEAD_DIM), lambda s: (group, jnp.maximum(s - n_cast, 0) // BATCH, 0))

    cos2, lo2, hi2 = tables
    return pl.pallas_call(
        functools.partial(_qkv_kernel, n_cast=n_cast, row_chunks=2),
        grid=(n_cast + T_LAT // tm,),
        in_specs=[
            pl.BlockSpec((tm, D_MODEL), lambda s: (tile(s), 0)),
            pl.BlockSpec((None, kc, QKV_W), lambda s: (0, jnp.minimum(s, n_cast - 1), 0)),
            table_spec(0), table_spec(0), table_spec(0),
            table_spec(1), table_spec(1), table_spec(1),
        ],
        out_specs=pl.BlockSpec((tm, QKV_W), lambda s: (tile(s), 0)),
        out_shape=jax.ShapeDtypeStruct((T_LAT, QKV_W), BF16),
        scratch_shapes=[pltpu.VMEM((D_MODEL, QKV_W), BF16)],
        compiler_params=_params(("arbitrary",)),
        name="qkv_latent",
    )(u, w_qkv, cos2, lo2, hi2, cos2, lo2, hi2)


def _proj_norm_kernel(*refs, n_src, n_cast, n_lat_tiles, row_chunks):
    x_ref, w_ref = refs[:2]
    h_refs = refs[2:2 + n_src]
    gate_ref, g_ref, sc_ref, sh_ref, h_out, u_out, wb_ref = refs[2 + n_src:]
    step = pl.program_id(0)
    kc = w_ref.shape[0]
    rc = x_ref.shape[0] // row_chunks

    @pl.when(step < n_cast)
    def _():
        wb_ref[pl.ds(pl.multiple_of(step * kc, kc), kc), :] = w_ref[...].astype(BF16)

    def compute(h_ref):
        for r in range(row_chunks):
            rows = pl.ds(r * rc, rc)
            acc = jnp.dot(x_ref[rows, :], wb_ref[...], preferred_element_type=F32)
            h_new = h_ref[rows, :] + gate_ref[...] * acc
            h_out[rows, :] = h_new
            u_out[rows, :] = _rms_mod(h_new, g_ref, sc_ref, sh_ref).astype(u_out.dtype)

    @pl.when(step >= n_cast)
    def _():
        _pick_stream(h_refs, step - n_cast, n_lat_tiles, compute)


def _proj_norm(x, w, layer, h_srcs, mod, g, rows):
    tm = 512
    n_cast = 4
    kc = D_MODEL // n_cast

    def tile(s):
        return jnp.maximum(s - n_cast, 0)

    def mod_spec(blk):
        return pl.BlockSpec((None, 1, D_MODEL), lambda s: (_mod_row(tile(s), tm), 0, blk))

    body = functools.partial(_proj_norm_kernel, n_src=len(h_srcs), n_cast=n_cast,
                             n_lat_tiles=T_LAT // tm, row_chunks=2)
    return pl.pallas_call(
        body,
        grid=(n_cast + rows // tm,),
        in_specs=[
            pl.BlockSpec((tm, D_MODEL), lambda s: (tile(s), 0)),
            pl.BlockSpec((None, kc, D_MODEL), lambda s: (layer, jnp.minimum(s, n_cast - 1), 0)),
        ] + _stream_specs(len(h_srcs), tm, tile) + [
            mod_spec(2),
            pl.BlockSpec((1, D_MODEL), lambda s: (0, 0)),
            mod_spec(4),
            mod_spec(3),
        ],
        out_specs=[pl.BlockSpec((tm, D_MODEL), lambda s: (tile(s), 0)),
                   pl.BlockSpec((tm, D_MODEL), lambda s: (tile(s), 0))],
        out_shape=[jax.ShapeDtypeStruct((rows, D_MODEL), F32),
                   jax.ShapeDtypeStruct((rows, D_MODEL), BF16)],
        scratch_shapes=[pltpu.VMEM((D_MODEL, D_MODEL), BF16)],
        compiler_params=_params(("arbitrary",)),
        name="proj_norm",
    )(x, w, *h_srcs, mod, g.reshape(1, D_MODEL), mod, mod)


HALO = 16


def _conv_gate_kernel(b_ref, z_ref, zp_ref, zn_ref, cw_ref, o_ref, *, tm):
    i = pl.program_id(0)
    seq_len = jnp.where(i * tm >= T_LAT, CTX_LEN, SEQ)
    sub = 8
    piece = CTX_LEN
    cw = 512
    for p in range(tm // piece):
        lo = p * piece
        row0 = i * tm + lo
        starts_seq = (row0 & (seq_len - 1)) == 0
        ends_seq = ((row0 + piece) & (seq_len - 1)) == 0
        for c in range(D_MODEL // cw):
            cols = slice(c * cw, (c + 1) * cw)
            before = zp_ref[:, cols] if p == 0 else z_ref[lo - HALO:lo, cols]
            after = zn_ref[:, cols] if lo + piece == tm else z_ref[lo + piece:lo + piece + HALO, cols]
            head = jnp.where(starts_seq, 0.0, before.astype(F32)[HALO - sub:, :])
            tail = jnp.where(ends_seq, 0.0, after.astype(F32)[:sub, :])
            zp = jnp.concatenate([head, z_ref[lo:lo + piece, cols].astype(F32), tail], axis=0)
            z_prev = pltpu.roll(zp, 1, 0)[sub:sub + piece, :]
            z_next = pltpu.roll(zp, piece + 2 * sub - 1, 0)[sub:sub + piece, :]
            z_mid = zp[sub:sub + piece, :]
            conv = (z_prev * cw_ref[pl.ds(0, 1), cols] + z_mid * cw_ref[pl.ds(1, 1), cols]
                    + z_next * cw_ref[pl.ds(2, 1), cols])
            o_ref[lo:lo + piece, cols] = (b_ref[lo:lo + piece, cols].astype(F32) * conv).astype(o_ref.dtype)


def _conv_gate(b_gate, z, conv_w, rows):
    tm = 1024
    assert tm % CTX_LEN == 0 and SEQ % tm == 0
    per = tm // HALO
    last = rows // HALO - 1
    z3 = z.reshape(rows // HALO, HALO, D_MODEL)
    return pl.pallas_call(
        functools.partial(_conv_gate_kernel, tm=tm),
        grid=(rows // tm,),
        in_specs=[
            pl.BlockSpec((tm, D_MODEL), lambda i: (i, 0)),
            pl.BlockSpec((tm, D_MODEL), lambda i: (i, 0)),
            pl.BlockSpec((None, HALO, D_MODEL), lambda i: (jnp.maximum(i * per - 1, 0), 0, 0)),
            pl.BlockSpec((None, HALO, D_MODEL), lambda i: (jnp.minimum((i + 1) * per, last), 0, 0)),
            pl.BlockSpec((3, D_MODEL), lambda i: (0, 0)),
        ],
        out_specs=pl.BlockSpec((tm, D_MODEL), lambda i: (i, 0)),
        out_shape=jax.ShapeDtypeStruct((rows, D_MODEL), BF16),
        compiler_params=_params(("parallel",)),
        name="conv_gate",
    )(b_gate, z, z3, z3, conv_w)


def _conv_mixer_kernel(x_ref, wb_ref, wc_ref, wh_ref, cw_ref, o_ref, sb_ref, sc_ref, sh_ref, zk_ref, bk_ref,
                       *, row_chunks):
    i = pl.program_id(1)
    tm, tn = o_ref.shape
    rc = tm // row_chunks
    sub = 8

    @pl.when(i == 0)
    def _():
        sb_ref[...] = wb_ref[...].astype(BF16)
        sc_ref[...] = wc_ref[...].astype(BF16)
        sh_ref[...] = wh_ref[...].astype(BF16)
        zk_ref[pl.ds(0, sub), :] = jnp.zeros((sub, tn), F32)
        zk_ref[pl.ds(sub + tm, sub), :] = jnp.zeros((sub, tn), F32)

    row0 = i * tm
    seq_len = jnp.where(row0 >= T_LAT, CTX_LEN, SEQ)

    def project(r):
        rows = pl.ds(r * rc, rc)
        x = x_ref[rows, :]
        bk_ref[rows, :] = jnp.dot(x, sb_ref[...], preferred_element_type=F32)
        zk_ref[pl.ds(sub + r * rc, rc), :] = (jnp.dot(x, sc_ref[...], preferred_element_type=F32)
                                               * jnp.dot(x, sh_ref[...], preferred_element_type=F32))

    def finish(r):
        base = r * rc
        pos = (row0 + base + lax.broadcasted_iota(jnp.int32, (rc, 1), 0)) & (seq_len - 1)
        zp = zk_ref[pl.ds(base, rc + 2 * sub), :]
        z_prev = jnp.where(pos == 0, 0.0, pltpu.roll(zp, 1, 0)[sub:sub + rc, :])
        z_next = jnp.where(pos == seq_len - 1, 0.0, pltpu.roll(zp, rc + 2 * sub - 1, 0)[sub:sub + rc, :])
        conv = (z_prev * cw_ref[pl.ds(0, 1), :] + zp[sub:sub + rc, :] * cw_ref[pl.ds(1, 1), :]
                + z_next * cw_ref[pl.ds(2, 1), :])
        o_ref[pl.ds(base, rc), :] = (bk_ref[pl.ds(base, rc), :] * conv).astype(o_ref.dtype)

    project(0)
    for r in range(1, row_chunks):
        project(r)
        finish(r - 1)
    finish(row_chunks - 1)


def _conv_mixer_in(u, w_in, conv_w, rows):
    tm, tn = SEQ, 512
    assert tm % CTX_LEN == 0 and T_LAT % tm == 0 and rows % tm == 0
    nblk = D_MODEL // tn

    def w_spec(part):
        return pl.BlockSpec((None, D_MODEL, tn), lambda j, i: (0, 0, j + part * nblk),
                            pipeline_mode=pl.Buffered(1))

    return pl.pallas_call(
        functools.partial(_conv_mixer_kernel, row_chunks=4),
        grid=(nblk, rows // tm),
        in_specs=[
            pl.BlockSpec((tm, D_MODEL), lambda j, i: (i, 0)),
            w_spec(0), w_spec(1), w_spec(2),
            pl.BlockSpec((None, 3, tn), lambda j, i: (0, 0, j)),
        ],
        out_specs=pl.BlockSpec((tm, tn), lambda j, i: (i, j)),
        out_shape=jax.ShapeDtypeStruct((rows, D_MODEL), BF16),
        scratch_shapes=[pltpu.VMEM((D_MODEL, tn), BF16) for _ in range(3)] + [
            pltpu.VMEM((tm + 16, tn), F32),
            pltpu.VMEM((tm, tn), F32),
        ],
        compiler_params=_params(("arbitrary", "arbitrary")),
        name="conv_mixer_in",
    )(u, w_in, w_in, w_in, conv_w)


def _attn_kernel(sink_ref, q_ref, k_ref, v_ref, kc_ref, vc_ref, o_ref, va_ref, vca_ref, kt_ref):
    kvh = pl.program_id(1)
    va_ref[:, :HEAD_DIM] = v_ref[...]
    va_ref[:, HEAD_DIM:] = jnp.ones((SEQ, HEAD_DIM), BF16)
    vca_ref[:, :HEAD_DIM] = vc_ref[...]
    vca_ref[:, HEAD_DIM:] = jnp.ones((CTX_LEN, HEAD_DIM), BF16)
    kt_ref[...] = k_ref[...].T
    kct = kc_ref[...].T
    vca = vca_ref[...]
    rel = (lax.broadcasted_iota(jnp.int32, (WINDOW, KWIN), 0)
           - lax.broadcasted_iota(jnp.int32, (WINDOW, KWIN), 1))
    masks = {}

    for n in range(SEQ // WINDOW):
        q0 = n * WINDOW
        start = min(max(q0 - WINDOW, 0), SEQ - KWIN)
        if q0 - start not in masks:
            masks[q0 - start] = jnp.abs(rel + (q0 - start)) <= WINDOW
        valid = masks[q0 - start]
        qs = jnp.concatenate(
            [q_ref[pl.ds(q0, WINDOW), g * HEAD_DIM:(g + 1) * HEAD_DIM] for g in range(GROUP)], axis=0)
        vwa = va_ref[pl.ds(start, KWIN), :]
        s_loc = jnp.dot(qs, kt_ref[:, start:start + KWIN], preferred_element_type=F32)
        s_ctx = jnp.dot(qs, kct, preferred_element_type=F32)
        p_loc, p_ctx, sink_terms = [], [], []
        for g in range(GROUP):
            sl = jnp.where(valid, s_loc[g * WINDOW:(g + 1) * WINDOW], NEG)
            sc = s_ctx[g * WINDOW:(g + 1) * WINDOW]
            sk = sink_ref[kvh * GROUP + g] * LOG2E
            m = jnp.maximum(jnp.maximum(jnp.max(sl, axis=-1, keepdims=True),
                                        jnp.max(sc, axis=-1, keepdims=True)), sk)
            p_loc.append(jnp.exp2(sl - m).astype(BF16))
            p_ctx.append(jnp.exp2(sc - m).astype(BF16))
            sink_terms.append(jnp.exp2(sk - m))
        oa = (jnp.dot(jnp.concatenate(p_loc, axis=0), vwa, preferred_element_type=F32)
              + jnp.dot(jnp.concatenate(p_ctx, axis=0), vca, preferred_element_type=F32))
        o = oa[:, :HEAD_DIM] / (oa[:, HEAD_DIM:] + jnp.concatenate(sink_terms, axis=0))
        for g in range(GROUP):
            o_ref[pl.ds(q0, WINDOW), g * HEAD_DIM:(g + 1) * HEAD_DIM] = (
                o[g * WINDOW:(g + 1) * WINDOW].astype(o_ref.dtype))


def _attention(qkv, kvc, sink):
    qw = GROUP * HEAD_DIM
    k_off = N_HEADS
    v_off = N_HEADS + N_KV_HEADS
    return pl.pallas_call(
        _attn_kernel,
        grid=(BATCH, N_KV_HEADS),
        in_specs=[
            pl.BlockSpec(memory_space=pltpu.SMEM),
            pl.BlockSpec((SEQ, qw), lambda b, h: (b, h)),
            pl.BlockSpec((SEQ, HEAD_DIM), lambda b, h: (b, k_off + h)),
            pl.BlockSpec((SEQ, HEAD_DIM), lambda b, h: (b, v_off + h)),
            pl.BlockSpec((CTX_LEN, HEAD_DIM), lambda b, h: (b, h)),
            pl.BlockSpec((CTX_LEN, HEAD_DIM), lambda b, h: (b, N_KV_HEADS + h)),
        ],
        out_specs=pl.BlockSpec((SEQ, qw), lambda b, h: (b, h)),
        out_shape=jax.ShapeDtypeStruct((T_LAT, N_HEADS * HEAD_DIM), BF16),
        scratch_shapes=[pltpu.VMEM((SEQ, 2 * HEAD_DIM), BF16), pltpu.VMEM((CTX_LEN, 2 * HEAD_DIM), BF16),
                        pltpu.VMEM((HEAD_DIM, SEQ), BF16)],
        compiler_params=_params(("parallel", "parallel")),
        name="window_attention",
    )(sink, qkv, qkv, qkv, kvc, kvc)


def _rope_tables():
    rows_n = SEQ // GRID_W
    row = jnp.repeat(jnp.arange(rows_n), GRID_W).astype(F32)
    col = jnp.tile(jnp.arange(GRID_W), rows_n).astype(F32)
    nf = HEAD_DIM // 4
    inv_freq = ROPE_BASE ** (-jnp.arange(nf, dtype=F32) / nf)
    ang_r = row[:, None] * inv_freq[None, :]
    ang_c = col[:, None] * inv_freq[None, :]
    ang = jnp.concatenate([ang_r, ang_r, ang_c, ang_c], axis=-1)
    cos, sin = jnp.cos(ang), jnp.sin(ang)
    low_half = (jnp.arange(HEAD_DIM) % (2 * nf)) < nf
    sin_lo = jnp.where(low_half[None, :], -sin, 0.0)
    sin_hi = jnp.where(low_half[None, :], 0.0, sin)
    scale = LOG2E / math.sqrt(HEAD_DIM)
    return (jnp.stack([cos * scale, cos]), jnp.stack([sin_lo * scale, sin_lo]),
            jnp.stack([sin_hi * scale, sin_hi]))


def _mlp(h, u2, mod, w1, w2, layer, g_next, mod_next, rows):
    tm = 2048
    a, w2_bf16 = _matmul_ws(u2, [(w1, layer, 0)], [], [(D_FF, BF16)], _epi_relu2,
                            rows=rows, tm=tm, tn=1024, n_cols=D_FF, side=(w2, layer, T_LAT // tm),
                            name="mlp_up")
    return _mlp_down(a, w2_bf16, h, mod, g_next, mod_next, rows=rows)


def kernel(x, c, ctx, c_ctx, norm1_g, norm2_g, mod_w, mod_b, conv_w_in, conv_w, conv_w_out,
           attn_w_qkv, attn_sink, attn_w_o, mlp_w1, mlp_w2, final_g):
    x_lat = x.reshape(T_LAT, D_MODEL)
    x_ctx = ctx.reshape(T_CTX, D_MODEL)
    cvec = jnp.concatenate(
        [c, c_ctx[None, :], jnp.zeros((MOD_ROWS - BATCH - 1, D_MODEL), F32)], axis=0)
    mod = _mod_table(cvec, mod_w, mod_b)

    m0 = mod[0]
    u = _norm_mod([x_lat, x_ctx], norm1_g[0], m0, 0, 1, T_ALL)
    v = _conv_mixer_in(u, conv_w_in, conv_w, T_ALL)
    h, u2 = _proj_norm(v, conv_w_out, 0, [x_lat, x_ctx], m0, norm2_g[0], T_ALL)
    m1 = mod[1]
    h, u = _mlp(h, u2, m0, mlp_w1, mlp_w2, 0, norm1_g[1], m1, T_ALL)

    w_qkv = attn_w_qkv
    qkv = _qkv_latent(u, w_qkv, _rope_tables())
    kv_cols = 2 * N_KV_HEADS * HEAD_DIM
    kvc = _matmul_ws(u, [(w_qkv, 0, N_HEADS * HEAD_DIM // 512)], [], [(kv_cols, BF16)], _epi_plain,
                     rows=T_CTX, tm=1024, tn=512, n_cols=kv_cols,
                     row_blk_off=T_LAT // 1024, name="kv_context")[0]
    o = _attention(qkv, kvc, attn_sink[0])
    h, u2 = _proj_norm(o, attn_w_o, 0, [h], m1, norm2_g[1], T_LAT)
    out = _mlp(h, u2, m1, mlp_w1, mlp_w2, 1, final_g, None, T_LAT)[0]
    return out.reshape(BATCH, SEQ, D_MODEL)
```

```python
import functools
import math

import jax
import jax.numpy as jnp
from jax import lax
from jax.experimental import pallas as pl
from jax.experimental.pallas import tpu as pltpu

F32 = jnp.float32
BF16 = jnp.bfloat16

D_MODEL = 2048
BATCH = 8
SEQ = 2048
CTX_LEN = 256
GRID_W = 64
HEAD_DIM = 128
N_HEADS = 16
N_KV_HEADS = 4
GROUP = 4
WINDOW = 128
ROPE_BASE = 10000.0
D_FF = 4 * D_MODEL
N_MOD = 6
EPS = 1e-6
NEG = -1e30
LOG2E = math.log2(math.e)

T_LAT = BATCH * SEQ
T_CTX = BATCH * CTX_LEN
T_ALL = T_LAT + T_CTX
MOD_ROWS = 16
QKV_W = (N_HEADS + 2 * N_KV_HEADS) * HEAD_DIM
KWIN = 3 * WINDOW

VMEM_LIMIT = 60 * 1024 * 1024


def _params(sem):
    return pltpu.CompilerParams(dimension_semantics=sem, vmem_limit_bytes=VMEM_LIMIT)


def _mod_row(i, tm):
    return jnp.minimum((i * tm) // SEQ, BATCH)


def _mod_kernel(c_ref, w_ref, b_ref, o_ref):
    s = jax.nn.silu(c_ref[...]).astype(BF16)
    o_ref[...] = jnp.dot(s, w_ref[...].astype(BF16), preferred_element_type=F32) + b_ref[...]


def _mod_table(cvec, mod_w, mod_b):
    depth = mod_w.shape[0]
    n = N_MOD * D_MODEL
    tn = 2048
    out = pl.pallas_call(
        _mod_kernel,
        grid=(depth, n // tn),
        in_specs=[
            pl.BlockSpec((MOD_ROWS, D_MODEL), lambda l, j: (0, 0)),
            pl.BlockSpec((None, D_MODEL, tn), lambda l, j: (l, 0, j)),
            pl.BlockSpec((None, 1, tn), lambda l, j: (l, 0, j)),
        ],
        out_specs=pl.BlockSpec((None, MOD_ROWS, tn), lambda l, j: (l, 0, j)),
        out_shape=jax.ShapeDtypeStruct((depth, MOD_ROWS, n), F32),
        compiler_params=_params(("parallel", "parallel")),
        name="mod_table",
    )(cvec, mod_w, mod_b.reshape(depth, 1, n))
    return out.reshape(depth, MOD_ROWS, 1, n)


def _rms_mod(x, g_ref, sc_ref, sh_ref):
    y = x * lax.rsqrt(jnp.mean(x * x, axis=-1, keepdims=True) + EPS)
    y = y * g_ref[...]
    return y * (1.0 + sc_ref[...]) + sh_ref[...]


def _pick_stream(refs, tile, n_lat_tiles, fn):
    if len(refs) == 1:
        fn(refs[0])
        return
    pl.when(tile < n_lat_tiles)(lambda: fn(refs[0]))
    pl.when(tile >= n_lat_tiles)(lambda: fn(refs[1]))


def _stream_specs(n_src, tm, tile_of):
    n_lat = T_LAT // tm
    if n_src == 1:
        return [pl.BlockSpec((tm, D_MODEL), lambda *g: (tile_of(*g), 0))]
    return [
        pl.BlockSpec((tm, D_MODEL), lambda *g: (jnp.minimum(tile_of(*g), n_lat - 1), 0)),
        pl.BlockSpec((tm, D_MODEL), lambda *g: (jnp.maximum(tile_of(*g) - n_lat, 0), 0)),
    ]


def _norm_mod_kernel(*refs, n_lat_tiles):
    srcs = refs[:-4]
    g_ref, sc_ref, sh_ref, o_ref = refs[-4:]

    def emit(x_ref):
        rc = 64
        for r in range(o_ref.shape[0] // rc):
            rows = pl.ds(r * rc, rc)
            o_ref[rows, :] = _rms_mod(x_ref[rows, :], g_ref, sc_ref, sh_ref).astype(o_ref.dtype)

    _pick_stream(srcs, pl.program_id(0), n_lat_tiles, emit)


def _norm_mod(srcs, g, mod, shift_blk, scale_blk, rows):
    tm = 1024
    return pl.pallas_call(
        functools.partial(_norm_mod_kernel, n_lat_tiles=T_LAT // tm),
        grid=(rows // tm,),
        in_specs=_stream_specs(len(srcs), tm, lambda i: i) + [
            pl.BlockSpec((1, D_MODEL), lambda i: (0, 0)),
            pl.BlockSpec((None, 1, D_MODEL), lambda i: (_mod_row(i, tm), 0, scale_blk)),
            pl.BlockSpec((None, 1, D_MODEL), lambda i: (_mod_row(i, tm), 0, shift_blk)),
        ],
        out_specs=pl.BlockSpec((tm, D_MODEL), lambda i: (i, 0)),
        out_shape=jax.ShapeDtypeStruct((rows, D_MODEL), BF16),
        compiler_params=_params(("parallel",)),
        name="norm_mod",
    )(*srcs, g.reshape(1, D_MODEL), mod, mod)


def _ws_kernel(*refs, nw, ne, no, ns, row_chunks, epilogue):
    refs = list(refs)
    x_ref = refs.pop(0)
    w_refs = [refs.pop(0) for _ in range(nw)]
    e_refs = [refs.pop(0) for _ in range(ne)]
    side_in = [refs.pop(0) for _ in range(ns)]
    o_refs = [refs.pop(0) for _ in range(no)]
    side_out = [refs.pop(0) for _ in range(ns)]
    strip_refs = refs

    @pl.when(pl.program_id(1) == 0)
    def _():
        for w_ref, strip in zip(w_refs, strip_refs):
            strip[...] = w_ref[...].astype(BF16)

    for src, dst in zip(side_in, side_out):
        dst[...] = src[...].astype(BF16)

    rc = x_ref.shape[0] // row_chunks
    for r in range(row_chunks):
        rows = pl.ds(r * rc, rc)
        x = x_ref[rows, :]
        accs = [jnp.dot(x, strip[...], preferred_element_type=F32) for strip in strip_refs]
        for o_ref, val in zip(o_refs, epilogue(accs, e_refs, rows)):
            o_ref[rows, :] = val.astype(o_ref.dtype)


def _matmul_ws(x, ws, extras, outs, epilogue, *, rows, tm, tn, n_cols, row_blk_off=0, row_chunks=1,
               side=None, name):
    kdim = x.shape[1]
    n_j, n_i = n_cols // tn, rows // tm
    in_specs = [pl.BlockSpec((tm, kdim), lambda j, i: (i + row_blk_off, 0))]
    args = [x]
    for w, layer, off in ws:
        in_specs.append(pl.BlockSpec((None, kdim, tn), lambda j, i, layer=layer, off=off: (layer, 0, j + off)))
        args.append(w)
    for arr, blk, imap in extras:
        in_specs.append(pl.BlockSpec(blk, lambda j, i, imap=imap: imap(i, j)))
        args.append(arr)
    out_specs = [pl.BlockSpec((tm, tn), lambda j, i: (i, j)) for _ in outs]
    out_shape = [jax.ShapeDtypeStruct((rows, cols), dt) for cols, dt in outs]
    if side is not None:
        side_w, side_layer, per_j = side
        _, side_rows, side_cols = side_w.shape
        rb = side_rows // (n_j * per_j)

        def side_blk(j, i):
            return j * per_j + jnp.minimum(i, per_j - 1)

        in_specs.append(pl.BlockSpec((None, rb, side_cols), lambda j, i: (side_layer, side_blk(j, i), 0)))
        args.append(side_w)
        out_specs.append(pl.BlockSpec((rb, side_cols), lambda j, i: (side_blk(j, i), 0)))
        out_shape.append(jax.ShapeDtypeStruct((side_rows, side_cols), BF16))
    body = functools.partial(_ws_kernel, nw=len(ws), ne=len(extras), no=len(outs),
                             ns=0 if side is None else 1, row_chunks=row_chunks, epilogue=epilogue)
    return pl.pallas_call(
        body,
        grid=(n_j, n_i),
        in_specs=in_specs,
        out_specs=out_specs,
        out_shape=out_shape,
        scratch_shapes=[pltpu.VMEM((kdim, tn), BF16) for _ in ws],
        compiler_params=_params(("arbitrary", "arbitrary")),
        name=name,
    )(*args)


def _mlp_down_kernel(*refs, nk, final, row_chunks):
    if final:
        x_ref, w_ref, h_ref, gate_ref, g_ref, out_ref, acc_ref = refs
        sc_ref = sh_ref = h_out = None
    else:
        x_ref, w_ref, h_ref, gate_ref, g_ref, sc_ref, sh_ref, h_out, out_ref, acc_ref = refs
    k = pl.program_id(1)
    j = pl.program_id(2)
    _, tm, tn = acc_ref.shape
    rc = tm // row_chunks

    @pl.when(k == 0)
    def _():
        acc_ref[j] = jnp.dot(x_ref[...], w_ref[...], preferred_element_type=F32)

    @pl.when(jnp.logical_and(k > 0, k < nk - 1))
    def _():
        acc_ref[j] += jnp.dot(x_ref[...], w_ref[...], preferred_element_type=F32)

    def h_new_rows(half, rows):
        acc = acc_ref[half, rows, :] + jnp.dot(x_ref[rows, :], w_ref[...], preferred_element_type=F32)
        h_new = h_ref[rows, :] + gate_ref[...] * acc
        if not final:
            h_out[rows, :] = h_new
        return h_new

    @pl.when(jnp.logical_and(k == nk - 1, j == 0))
    def _():
        for r in range(row_chunks):
            rows = pl.ds(r * rc, rc)
            acc_ref[0, rows, :] = h_new_rows(0, rows)

    @pl.when(jnp.logical_and(k == nk - 1, j == 1))
    def _():
        for r in range(row_chunks):
            rows = pl.ds(r * rc, rc)
            halves = (acc_ref[0, rows, :], h_new_rows(1, rows))
            ssq = sum(jnp.sum(hv * hv, axis=-1, keepdims=True) for hv in halves)
            rstd = lax.rsqrt(ssq * (1.0 / D_MODEL) + EPS)
            for half, hv in enumerate(halves):
                cols = slice(half * tn, (half + 1) * tn)
                y = hv * rstd * g_ref[:, cols]
                if not final:
                    y = y * (1.0 + sc_ref[:, cols]) + sh_ref[:, cols]
                out_ref[rows, cols] = y.astype(out_ref.dtype)


def _mlp_down(x, w, h, mod, g, mod_next, *, rows):
    tm, tk = 1024, 2048
    tn = D_MODEL // 2
    nk = D_FF // tk
    final = mod_next is None

    def half_of(k, j):
        return jnp.where(k == nk - 1, j, 0)

    def mod_spec(blk, width):
        return pl.BlockSpec((None, 1, width), lambda i, k, j: (_mod_row(i, tm), 0, blk))

    in_specs = [
        pl.BlockSpec((tm, tk), lambda i, k, j: (i, k)),
        pl.BlockSpec((tk, tn), lambda i, k, j: (k, j)),
        pl.BlockSpec((tm, tn), lambda i, k, j: (i, half_of(k, j))),
        pl.BlockSpec((None, 1, tn), lambda i, k, j: (_mod_row(i, tm), 0, 5 * 2 + half_of(k, j))),
        pl.BlockSpec((1, D_MODEL), lambda i, k, j: (0, 0)),
    ]
    args = [x, w, h, mod, g.reshape(1, D_MODEL)]
    full_rows = pl.BlockSpec((tm, D_MODEL), lambda i, k, j: (i, 0))
    if final:
        out_specs = [full_rows]
        out_shape = [jax.ShapeDtypeStruct((rows, D_MODEL), F32)]
    else:
        in_specs += [mod_spec(1, D_MODEL), mod_spec(0, D_MODEL)]
        args += [mod_next, mod_next]
        out_specs = [pl.BlockSpec((tm, tn), lambda i, k, j: (i, half_of(k, j))), full_rows]
        out_shape = [jax.ShapeDtypeStruct((rows, D_MODEL), F32), jax.ShapeDtypeStruct((rows, D_MODEL), BF16)]
    return pl.pallas_call(
        functools.partial(_mlp_down_kernel, nk=nk, final=final, row_chunks=4),
        grid=(rows // tm, nk, 2),
        in_specs=in_specs,
        out_specs=out_specs,
        out_shape=out_shape,
        scratch_shapes=[pltpu.VMEM((2, tm, tn), F32)],
        compiler_params=_params(("arbitrary", "arbitrary", "arbitrary")),
        name="mlp_down",
    )(*args)


def _epi_relu2(accs, e_refs, rows):
    a = jnp.maximum(accs[0], 0.0)
    return [a * a]


def _epi_plain(accs, e_refs, rows):
    return [accs[0]]


def _rope(a, cos, sin_lo, sin_hi):
    up = pltpu.roll(a, HEAD_DIM - HEAD_DIM // 4, 1)
    dn = pltpu.roll(a, HEAD_DIM // 4, 1)
    return a * cos + up * sin_lo + dn * sin_hi


def _qkv_kernel(x_ref, w_ref, cq_ref, lq_ref, hq_ref, ck_ref, lk_ref, hk_ref, o_ref, wb_ref,
                *, n_cast, row_chunks):
    step = pl.program_id(0)
    kc = w_ref.shape[0]
    rc = x_ref.shape[0] // row_chunks

    @pl.when(step < n_cast)
    def _():
        wb_ref[pl.ds(pl.multiple_of(step * kc, kc), kc), :] = w_ref[...].astype(BF16)

    @pl.when(step >= n_cast)
    def _():
        for r in range(row_chunks):
            rows = pl.ds(r * rc, rc)
            acc = jnp.dot(x_ref[rows, :], wb_ref[...], preferred_element_type=F32)
            q_tabs = (cq_ref[rows, :], lq_ref[rows, :], hq_ref[rows, :])
            k_tabs = (ck_ref[rows, :], lk_ref[rows, :], hk_ref[rows, :])
            for hh in range(N_HEADS + N_KV_HEADS):
                cols = slice(hh * HEAD_DIM, (hh + 1) * HEAD_DIM)
                tabs = q_tabs if hh < N_HEADS else k_tabs
                o_ref[rows, cols] = _rope(acc[:, cols], *tabs).astype(o_ref.dtype)
            v_cols = slice((N_HEADS + N_KV_HEADS) * HEAD_DIM, QKV_W)
            o_ref[rows, v_cols] = acc[:, v_cols].astype(o_ref.dtype)


def _qkv_latent(u, w_qkv, tables):
    tm = 512
    n_cast = 4
    kc = D_MODEL // n_cast
    pos_blocks = SEQ // tm

    def tile(s):
        t = jnp.maximum(s - n_cast, 0)
        return (t % BATCH) * pos_blocks + t // BATCH

    def table_spec(group):
        return pl.BlockSpec((None, tm, HEAD_DIM), lambda s: (group, jnp.maximum(s - n_cast, 0) // BATCH, 0))

    cos2, lo2, hi2 = tables
    return pl.pallas_call(
        functools.partial(_qkv_kernel, n_cast=n_cast, row_chunks=2),
        grid=(n_cast + T_LAT // tm,),
        in_specs=[
            pl.BlockSpec((tm, D_MODEL), lambda s: (tile(s), 0)),
            pl.BlockSpec((None, kc, QKV_W), lambda s: (0, jnp.minimum(s, n_cast - 1), 0)),
            table_spec(0), table_spec(0), table_spec(0),
            table_spec(1), table_spec(1), table_spec(1),
        ],
        out_specs=pl.BlockSpec((tm, QKV_W), lambda s: (tile(s), 0)),
        out_shape=jax.ShapeDtypeStruct((T_LAT, QKV_W), BF16),
        scratch_shapes=[pltpu.VMEM((D_MODEL, QKV_W), BF16)],
        compiler_params=_params(("arbitrary",)),
        name="qkv_latent",
    )(u, w_qkv, cos2, lo2, hi2, cos2, lo2, hi2)


def _proj_norm_kernel(*refs, n_src, n_cast, n_lat_tiles, row_chunks):
    x_ref, w_ref = refs[:2]
    h_refs = refs[2:2 + n_src]
    gate_ref, g_ref, sc_ref, sh_ref, h_out, u_out, wb_ref = refs[2 + n_src:]
    step = pl.program_id(0)
    kc = w_ref.shape[0]
    rc = x_ref.shape[0] // row_chunks

    @pl.when(step < n_cast)
    def _():
        wb_ref[pl.ds(pl.multiple_of(step * kc, kc), kc), :] = w_ref[...].astype(BF16)

    def compute(h_ref):
        for r in range(row_chunks):
            rows = pl.ds(r * rc, rc)
            acc = jnp.dot(x_ref[rows, :], wb_ref[...], preferred_element_type=F32)
            h_new = h_ref[rows, :] + gate_ref[...] * acc
            h_out[rows, :] = h_new
            u_out[rows, :] = _rms_mod(h_new, g_ref, sc_ref, sh_ref).astype(u_out.dtype)

    @pl.when(step >= n_cast)
    def _():
        _pick_stream(h_refs, step - n_cast, n_lat_tiles, compute)


def _proj_norm(x, w, layer, h_srcs, mod, g, rows):
    tm = 512
    n_cast = 4
    kc = D_MODEL // n_cast

    def tile(s):
        return jnp.maximum(s - n_cast, 0)

    def mod_spec(blk):
        return pl.BlockSpec((None, 1, D_MODEL), lambda s: (_mod_row(tile(s), tm), 0, blk))

    body = functools.partial(_proj_norm_kernel, n_src=len(h_srcs), n_cast=n_cast,
                             n_lat_tiles=T_LAT // tm, row_chunks=2)
    return pl.pallas_call(
        body,
        grid=(n_cast + rows // tm,),
        in_specs=[
            pl.BlockSpec((tm, D_MODEL), lambda s: (tile(s), 0)),
            pl.BlockSpec((None, kc, D_MODEL), lambda s: (layer, jnp.minimum(s, n_cast - 1), 0)),
        ] + _stream_specs(len(h_srcs), tm, tile) + [
            mod_spec(2),
            pl.BlockSpec((1, D_MODEL), lambda s: (0, 0)),
            mod_spec(4),
            mod_spec(3),
        ],
        out_specs=[pl.BlockSpec((tm, D_MODEL), lambda s: (tile(s), 0)),
                   pl.BlockSpec((tm, D_MODEL), lambda s: (tile(s), 0))],
        out_shape=[jax.ShapeDtypeStruct((rows, D_MODEL), F32),
                   jax.ShapeDtypeStruct((rows, D_MODEL), BF16)],
        scratch_shapes=[pltpu.VMEM((D_MODEL, D_MODEL), BF16)],
        compiler_params=_params(("arbitrary",)),
        name="proj_norm",
    )(x, w, *h_srcs, mod, g.reshape(1, D_MODEL), mod, mod)


def _conv_mixer_kernel(x_ref, wb_ref, wc_ref, wh_ref, cw_ref, o_ref, sb_ref, sc_ref, sh_ref, zk_ref, bk_ref,
                       *, row_chunks):
    i = pl.program_id(1)
    tm, tn = o_ref.shape
    rc = tm // row_chunks
    sub = 8

    @pl.when(i == 0)
    def _():
        sb_ref[...] = wb_ref[...].astype(BF16)
        sc_ref[...] = wc_ref[...].astype(BF16)
        sh_ref[...] = wh_ref[...].astype(BF16)
        zk_ref[pl.ds(0, sub), :] = jnp.zeros((sub, tn), F32)
        zk_ref[pl.ds(sub + tm, sub), :] = jnp.zeros((sub, tn), F32)

    row0 = i * tm
    seq_len = jnp.where(row0 >= T_LAT, CTX_LEN, SEQ)

    def project(r):
        rows = pl.ds(r * rc, rc)
        x = x_ref[rows, :]
        bk_ref[rows, :] = jnp.dot(x, sb_ref[...], preferred_element_type=F32)
        zk_ref[pl.ds(sub + r * rc, rc), :] = (jnp.dot(x, sc_ref[...], preferred_element_type=F32)
                                               * jnp.dot(x, sh_ref[...], preferred_element_type=F32))

    def finish(r):
        base = r * rc
        pos = (row0 + base + lax.broadcasted_iota(jnp.int32, (rc, 1), 0)) & (seq_len - 1)
        zp = zk_ref[pl.ds(base, rc + 2 * sub), :]
        z_prev = jnp.where(pos == 0, 0.0, pltpu.roll(zp, 1, 0)[sub:sub + rc, :])
        z_next = jnp.where(pos == seq_len - 1, 0.0, pltpu.roll(zp, rc + 2 * sub - 1, 0)[sub:sub + rc, :])
        conv = (z_prev * cw_ref[pl.ds(0, 1), :] + zp[sub:sub + rc, :] * cw_ref[pl.ds(1, 1), :]
                + z_next * cw_ref[pl.ds(2, 1), :])
        o_ref[pl.ds(base, rc), :] = (bk_ref[pl.ds(base, rc), :] * conv).astype(o_ref.dtype)

    project(0)
    for r in range(1, row_chunks):
        project(r)
        finish(r - 1)
    finish(row_chunks - 1)


def _conv_mixer_in(u, w_in, conv_w, rows):
    tm, tn = SEQ, 512
    assert tm % CTX_LEN == 0 and T_LAT % tm == 0 and rows % tm == 0
    nblk = D_MODEL // tn

    def w_spec(part):
        return pl.BlockSpec((None, D_MODEL, tn), lambda j, i: (0, 0, j + part * nblk),
                            pipeline_mode=pl.Buffered(1))

    return pl.pallas_call(
        functools.partial(_conv_mixer_kernel, row_chunks=4),
        grid=(nblk, rows // tm),
        in_specs=[
            pl.BlockSpec((tm, D_MODEL), lambda j, i: (i, 0)),
            w_spec(0), w_spec(1), w_spec(2),
            pl.BlockSpec((None, 3, tn), lambda j, i: (0, 0, j)),
        ],
        out_specs=pl.BlockSpec((tm, tn), lambda j, i: (i, j)),
        out_shape=jax.ShapeDtypeStruct((rows, D_MODEL), BF16),
        scratch_shapes=[pltpu.VMEM((D_MODEL, tn), BF16) for _ in range(3)] + [
            pltpu.VMEM((tm + 16, tn), F32),
            pltpu.VMEM((tm, tn), F32),
        ],
        compiler_params=_params(("arbitrary", "arbitrary")),
        name="conv_mixer_in",
    )(u, w_in, w_in, w_in, conv_w)


def _attn_kernel(sink_ref, q_ref, k_ref, v_ref, kc_ref, vc_ref, o_ref, va_ref, vca_ref, kt_ref):
    kvh = pl.program_id(1)
    va_ref[:, :HEAD_DIM] = v_ref[...]
    va_ref[:, HEAD_DIM:] = jnp.ones((SEQ, HEAD_DIM), BF16)
    vca_ref[:, :HEAD_DIM] = vc_ref[...]
    vca_ref[:, HEAD_DIM:] = jnp.ones((CTX_LEN, HEAD_DIM), BF16)
    kt_ref[...] = k_ref[...].T
    kct = kc_ref[...].T
    vca = vca_ref[...]
    rel = (lax.broadcasted_iota(jnp.int32, (WINDOW, KWIN), 0)
           - lax.broadcasted_iota(jnp.int32, (WINDOW, KWIN), 1))
    masks = {}

    for n in range(SEQ // WINDOW):
        q0 = n * WINDOW
        start = min(max(q0 - WINDOW, 0), SEQ - KWIN)
        if q0 - start not in masks:
            masks[q0 - start] = jnp.abs(rel + (q0 - start)) <= WINDOW
        valid = masks[q0 - start]
        vwa = va_ref[pl.ds(start, KWIN), :]
        p_loc, p_ctx, sink_terms = [], [], []
        for g in range(GROUP):
            qg = q_ref[pl.ds(q0, WINDOW), g * HEAD_DIM:(g + 1) * HEAD_DIM]
            sl = jnp.where(valid, jnp.dot(qg, kt_ref[:, start:start + KWIN], preferred_element_type=F32), NEG)
            sc = jnp.dot(qg, kct, preferred_element_type=F32)
            sk = sink_ref[kvh * GROUP + g] * LOG2E
            m = jnp.maximum(jnp.max(jnp.concatenate([sl, sc], axis=1), axis=-1, keepdims=True), sk)
            p_loc.append(jnp.exp2(sl - m).astype(BF16))
            p_ctx.append(jnp.exp2(sc - m).astype(BF16))
            sink_terms.append(jnp.exp2(sk - m))
        oa = (jnp.dot(jnp.concatenate(p_loc, axis=0), vwa, preferred_element_type=F32)
              + jnp.dot(jnp.concatenate(p_ctx, axis=0), vca, preferred_element_type=F32))
        o = oa[:, :HEAD_DIM] / (oa[:, HEAD_DIM:] + jnp.concatenate(sink_terms, axis=0))
        for g in range(GROUP):
            o_ref[pl.ds(q0, WINDOW), g * HEAD_DIM:(g + 1) * HEAD_DIM] = (
                o[g * WINDOW:(g + 1) * WINDOW].astype(o_ref.dtype))


def _attention(qkv, kvc, sink):
    qw = GROUP * HEAD_DIM
    k_off = N_HEADS
    v_off = N_HEADS + N_KV_HEADS
    return pl.pallas_call(
        _attn_kernel,
        grid=(BATCH, N_KV_HEADS),
        in_specs=[
            pl.BlockSpec(memory_space=pltpu.SMEM),
            pl.BlockSpec((SEQ, qw), lambda b, h: (b, h)),
            pl.BlockSpec((SEQ, HEAD_DIM), lambda b, h: (b, k_off + h)),
            pl.BlockSpec((SEQ, HEAD_DIM), lambda b, h: (b, v_off + h)),
            pl.BlockSpec((CTX_LEN, HEAD_DIM), lambda b, h: (b, h)),
            pl.BlockSpec((CTX_LEN, HEAD_DIM), lambda b, h: (b, N_KV_HEADS + h)),
        ],
        out_specs=pl.BlockSpec((SEQ, qw), lambda b, h: (b, h)),
        out_shape=jax.ShapeDtypeStruct((T_LAT, N_HEADS * HEAD_DIM), BF16),
        scratch_shapes=[pltpu.VMEM((SEQ, 2 * HEAD_DIM), BF16), pltpu.VMEM((CTX_LEN, 2 * HEAD_DIM), BF16),
                        pltpu.VMEM((HEAD_DIM, SEQ), BF16)],
        compiler_params=_params(("parallel", "parallel")),
        name="window_attention",
    )(sink, qkv, qkv, qkv, kvc, kvc)


def _rope_tables():
    rows_n = SEQ // GRID_W
    row = jnp.repeat(jnp.arange(rows_n), GRID_W).astype(F32)
    col = jnp.tile(jnp.arange(GRID_W), rows_n).astype(F32)
    nf = HEAD_DIM // 4
    inv_freq = ROPE_BASE ** (-jnp.arange(nf, dtype=F32) / nf)
    ang_r = row[:, None] * inv_freq[None, :]
    ang_c = col[:, None] * inv_freq[None, :]
    ang = jnp.concatenate([ang_r, ang_r, ang_c, ang_c], axis=-1)
    cos, sin = jnp.cos(ang), jnp.sin(ang)
    low_half = (jnp.arange(HEAD_DIM) % (2 * nf)) < nf
    sin_lo = jnp.where(low_half[None, :], -sin, 0.0)
    sin_hi = jnp.where(low_half[None, :], 0.0, sin)
    scale = LOG2E / math.sqrt(HEAD_DIM)
    return (jnp.stack([cos * scale, cos]), jnp.stack([sin_lo * scale, sin_lo]),
            jnp.stack([sin_hi * scale, sin_hi]))


def _mlp(h, u2, mod, w1, w2, layer, g_next, mod_next, rows):
    tm = 2048
    a, w2_bf16 = _matmul_ws(u2, [(w1, layer, 0)], [], [(D_FF, BF16)], _epi_relu2,
                            rows=rows, tm=tm, tn=1024, n_cols=D_FF, side=(w2, layer, T_LAT // tm),
                            name="mlp_up")
    return _mlp_down(a, w2_bf16, h, mod, g_next, mod_next, rows=rows)


def kernel(x, c, ctx, c_ctx, norm1_g, norm2_g, mod_w, mod_b, conv_w_in, conv_w, conv_w_out,
           attn_w_qkv, attn_sink, attn_w_o, mlp_w1, mlp_w2, final_g):
    x_lat = x.reshape(T_LAT, D_MODEL)
    x_ctx = ctx.reshape(T_CTX, D_MODEL)
    cvec = jnp.concatenate(
        [c, c_ctx[None, :], jnp.zeros((MOD_ROWS - BATCH - 1, D_MODEL), F32)], axis=0)
    mod = _mod_table(cvec, mod_w, mod_b)

    m0 = mod[0]
    u = _norm_mod([x_lat, x_ctx], norm1_g[0], m0, 0, 1, T_ALL)
    v = _conv_mixer_in(u, conv_w_in, conv_w, T_ALL)
    h, u2 = _proj_norm(v, conv_w_out, 0, [x_lat, x_ctx], m0, norm2_g[0], T_ALL)
    m1 = mod[1]
    h, u = _mlp(h, u2, m0, mlp_w1, mlp_w2, 0, norm1_g[1], m1, T_ALL)

    w_qkv = attn_w_qkv
    qkv = _qkv_latent(u, w_qkv, _rope_tables())
    kv_cols = 2 * N_KV_HEADS * HEAD_DIM
    kvc = _matmul_ws(u, [(w_qkv, 0, N_HEADS * HEAD_DIM // 512)], [], [(kv_cols, BF16)], _epi_plain,
                     rows=T_CTX, tm=1024, tn=512, n_cols=kv_cols,
                     row_blk_off=T_LAT // 1024, name="kv_context")[0]
    o = _attention(qkv, kvc, attn_sink[0])
    h, u2 = _proj_norm(o, attn_w_o, 0, [h], m1, norm2_g[1], T_LAT)
    out = _mlp(h, u2, m1, mlp_w1, mlp_w2, 1, final_g, None, T_LAT)[0]
    return out.reshape(BATCH, SEQ, D_MODEL)
```

```python
import functools
import math

import jax
import jax.numpy as jnp
from jax import lax
from jax.experimental import pallas as pl
from jax.experimental.pallas import tpu as pltpu

F32 = jnp.float32
BF16 = jnp.bfloat16

D_MODEL = 2048
BATCH = 8
SEQ = 2048
CTX_LEN = 256
GRID_W = 64
HEAD_DIM = 128
N_HEADS = 16
N_KV_HEADS = 4
GROUP = 4
WINDOW = 128
ROPE_BASE = 10000.0
D_FF = 4 * D_MODEL
N_MOD = 6
EPS = 1e-6
NEG = -1e30
LOG2E = math.log2(math.e)

T_LAT = BATCH * SEQ
T_CTX = BATCH * CTX_LEN
T_ALL = T_LAT + T_CTX
MOD_ROWS = 16
QKV_W = (N_HEADS + 2 * N_KV_HEADS) * HEAD_DIM
KWIN = 3 * WINDOW

F32_SUBLANES = 8
VMEM_LIMIT = 60 * 1024 * 1024

MOD_TN = 2048
NORM_TM = 1024
NORM_PASS_ROWS = 64
CONV_TILE = (SEQ, 512)
CONV_ROW_CHUNKS = 4
PROJ_TM = 512
QKV_TM = 512
CAST_STEPS = 4
ROW_CHUNKS = 2
MLP_UP_TILE = (2048, 1024)
MLP_DOWN_TILE = (1024, 2048)
MLP_DOWN_ROW_CHUNKS = 4
KV_CTX_TILE = (1024, 512)


def _params(sem):
    return pltpu.CompilerParams(dimension_semantics=sem, vmem_limit_bytes=VMEM_LIMIT)


def _mod_row(i, tm):
    return jnp.minimum((i * tm) // SEQ, BATCH)


def _mod_kernel(c_ref, w_ref, b_ref, o_ref):
    s = jax.nn.silu(c_ref[...]).astype(BF16)
    o_ref[...] = jnp.dot(s, w_ref[...].astype(BF16), preferred_element_type=F32) + b_ref[...]


def _mod_table(cvec, mod_w, mod_b):
    depth = mod_w.shape[0]
    n = N_MOD * D_MODEL
    tn = MOD_TN
    out = pl.pallas_call(
        _mod_kernel,
        grid=(depth, n // tn),
        in_specs=[
            pl.BlockSpec((MOD_ROWS, D_MODEL), lambda l, j: (0, 0)),
            pl.BlockSpec((None, D_MODEL, tn), lambda l, j: (l, 0, j)),
            pl.BlockSpec((None, 1, tn), lambda l, j: (l, 0, j)),
        ],
        out_specs=pl.BlockSpec((None, MOD_ROWS, tn), lambda l, j: (l, 0, j)),
        out_shape=jax.ShapeDtypeStruct((depth, MOD_ROWS, n), F32),
        compiler_params=_params(("parallel", "parallel")),
        name="mod_table",
    )(cvec, mod_w, mod_b.reshape(depth, 1, n))
    return out.reshape(depth, MOD_ROWS, 1, n)


def _rms_mod(x, g_ref, sc_ref, sh_ref):
    y = x * lax.rsqrt(jnp.mean(x * x, axis=-1, keepdims=True) + EPS)
    y = y * g_ref[...]
    return y * (1.0 + sc_ref[...]) + sh_ref[...]


def _pick_stream(refs, tile, n_lat_tiles, fn):
    if len(refs) == 1:
        fn(refs[0])
        return
    pl.when(tile < n_lat_tiles)(lambda: fn(refs[0]))
    pl.when(tile >= n_lat_tiles)(lambda: fn(refs[1]))


def _stream_specs(n_src, tm, tile_of):
    n_lat = T_LAT // tm
    if n_src == 1:
        return [pl.BlockSpec((tm, D_MODEL), lambda *g: (tile_of(*g), 0))]
    return [
        pl.BlockSpec((tm, D_MODEL), lambda *g: (jnp.minimum(tile_of(*g), n_lat - 1), 0)),
        pl.BlockSpec((tm, D_MODEL), lambda *g: (jnp.maximum(tile_of(*g) - n_lat, 0), 0)),
    ]


def _norm_mod_kernel(*refs, n_lat_tiles):
    srcs = refs[:-4]
    g_ref, sc_ref, sh_ref, o_ref = refs[-4:]

    def emit(x_ref):
        rc = NORM_PASS_ROWS
        for r in range(o_ref.shape[0] // rc):
            rows = pl.ds(r * rc, rc)
            o_ref[rows, :] = _rms_mod(x_ref[rows, :], g_ref, sc_ref, sh_ref).astype(o_ref.dtype)

    _pick_stream(srcs, pl.program_id(0), n_lat_tiles, emit)


def _norm_mod(srcs, g, mod, shift_blk, scale_blk, rows):
    tm = NORM_TM
    return pl.pallas_call(
        functools.partial(_norm_mod_kernel, n_lat_tiles=T_LAT // tm),
        grid=(rows // tm,),
        in_specs=_stream_specs(len(srcs), tm, lambda i: i) + [
            pl.BlockSpec((1, D_MODEL), lambda i: (0, 0)),
            pl.BlockSpec((None, 1, D_MODEL), lambda i: (_mod_row(i, tm), 0, scale_blk)),
            pl.BlockSpec((None, 1, D_MODEL), lambda i: (_mod_row(i, tm), 0, shift_blk)),
        ],
        out_specs=pl.BlockSpec((tm, D_MODEL), lambda i: (i, 0)),
        out_shape=jax.ShapeDtypeStruct((rows, D_MODEL), BF16),
        compiler_params=_params(("parallel",)),
        name="norm_mod",
    )(*srcs, g.reshape(1, D_MODEL), mod, mod)


def _ws_kernel(*refs, has_rider, epilogue):
    if has_rider:
        x_ref, w_ref, rider_in, o_ref, rider_out, strip_ref = refs
        rider_out[...] = rider_in[...].astype(BF16)
    else:
        x_ref, w_ref, o_ref, strip_ref = refs

    @pl.when(pl.program_id(1) == 0)
    def _():
        strip_ref[...] = w_ref[...].astype(BF16)

    acc = jnp.dot(x_ref[...], strip_ref[...], preferred_element_type=F32)
    o_ref[...] = epilogue(acc).astype(o_ref.dtype)


def _matmul_ws(x, w, layer, col_blk_off, epilogue, *, rows, tm, tn, n_cols, row_blk_off=0, rider=None, name):
    kdim = x.shape[1]
    n_j, n_i = n_cols // tn, rows // tm
    in_specs = [
        pl.BlockSpec((tm, kdim), lambda j, i: (i + row_blk_off, 0)),
        pl.BlockSpec((None, kdim, tn), lambda j, i: (layer, 0, j + col_blk_off)),
    ]
    args = [x, w]
    out_specs = [pl.BlockSpec((tm, tn), lambda j, i: (i, j))]
    out_shape = [jax.ShapeDtypeStruct((rows, n_cols), BF16)]
    if rider is not None:
        rider_w, rider_layer, per_j = rider
        _, rider_rows, rider_cols = rider_w.shape
        rb = rider_rows // (n_j * per_j)

        def rider_blk(j, i):
            return j * per_j + jnp.minimum(i, per_j - 1)

        in_specs.append(pl.BlockSpec((None, rb, rider_cols), lambda j, i: (rider_layer, rider_blk(j, i), 0)))
        args.append(rider_w)
        out_specs.append(pl.BlockSpec((rb, rider_cols), lambda j, i: (rider_blk(j, i), 0)))
        out_shape.append(jax.ShapeDtypeStruct((rider_rows, rider_cols), BF16))
    return pl.pallas_call(
        functools.partial(_ws_kernel, has_rider=rider is not None, epilogue=epilogue),
        grid=(n_j, n_i),
        in_specs=in_specs,
        out_specs=out_specs,
        out_shape=out_shape,
        scratch_shapes=[pltpu.VMEM((kdim, tn), BF16)],
        compiler_params=_params(("arbitrary", "arbitrary")),
        name=name,
    )(*args)


def _mlp_down_kernel(*refs, nk, final, row_chunks):
    if final:
        x_ref, w_ref, h_ref, gate_ref, g_ref, out_ref, acc_ref = refs
        sc_ref = sh_ref = h_out = None
    else:
        x_ref, w_ref, h_ref, gate_ref, g_ref, sc_ref, sh_ref, h_out, out_ref, acc_ref = refs
    k = pl.program_id(1)
    j = pl.program_id(2)
    _, tm, tn = acc_ref.shape
    rc = tm // row_chunks

    @pl.when(k == 0)
    def _():
        acc_ref[j] = jnp.dot(x_ref[...], w_ref[...], preferred_element_type=F32)

    @pl.when(jnp.logical_and(k > 0, k < nk - 1))
    def _():
        acc_ref[j] += jnp.dot(x_ref[...], w_ref[...], preferred_element_type=F32)

    def h_new_rows(half, rows):
        acc = acc_ref[half, rows, :] + jnp.dot(x_ref[rows, :], w_ref[...], preferred_element_type=F32)
        h_new = h_ref[rows, :] + gate_ref[...] * acc
        if not final:
            h_out[rows, :] = h_new
        return h_new

    @pl.when(jnp.logical_and(k == nk - 1, j == 0))
    def _():
        for r in range(row_chunks):
            rows = pl.ds(r * rc, rc)
            acc_ref[0, rows, :] = h_new_rows(0, rows)

    @pl.when(jnp.logical_and(k == nk - 1, j == 1))
    def _():
        for r in range(row_chunks):
            rows = pl.ds(r * rc, rc)
            halves = (acc_ref[0, rows, :], h_new_rows(1, rows))
            ssq = sum(jnp.sum(hv * hv, axis=-1, keepdims=True) for hv in halves)
            rstd = lax.rsqrt(ssq * (1.0 / D_MODEL) + EPS)
            for half, hv in enumerate(halves):
                cols = slice(half * tn, (half + 1) * tn)
                y = hv * rstd * g_ref[:, cols]
                if not final:
                    y = y * (1.0 + sc_ref[:, cols]) + sh_ref[:, cols]
                out_ref[rows, cols] = y.astype(out_ref.dtype)


def _mlp_down(x, w, h, mod, g, mod_next, *, rows):
    tm, tk = MLP_DOWN_TILE
    tn = D_MODEL // 2
    nk = D_FF // tk
    final = mod_next is None

    def half_of(k, j):
        return jnp.where(k == nk - 1, j, 0)

    def mod_spec(blk, width):
        return pl.BlockSpec((None, 1, width), lambda i, k, j: (_mod_row(i, tm), 0, blk))

    in_specs = [
        pl.BlockSpec((tm, tk), lambda i, k, j: (i, k)),
        pl.BlockSpec((tk, tn), lambda i, k, j: (k, j)),
        pl.BlockSpec((tm, tn), lambda i, k, j: (i, half_of(k, j))),
        pl.BlockSpec((None, 1, tn), lambda i, k, j: (_mod_row(i, tm), 0, 5 * 2 + half_of(k, j))),
        pl.BlockSpec((1, D_MODEL), lambda i, k, j: (0, 0)),
    ]
    args = [x, w, h, mod, g.reshape(1, D_MODEL)]
    full_rows = pl.BlockSpec((tm, D_MODEL), lambda i, k, j: (i, 0))
    if final:
        out_specs = [full_rows]
        out_shape = [jax.ShapeDtypeStruct((rows, D_MODEL), F32)]
    else:
        in_specs += [mod_spec(1, D_MODEL), mod_spec(0, D_MODEL)]
        args += [mod_next, mod_next]
        out_specs = [pl.BlockSpec((tm, tn), lambda i, k, j: (i, half_of(k, j))), full_rows]
        out_shape = [jax.ShapeDtypeStruct((rows, D_MODEL), F32), jax.ShapeDtypeStruct((rows, D_MODEL), BF16)]
    return pl.pallas_call(
        functools.partial(_mlp_down_kernel, nk=nk, final=final, row_chunks=MLP_DOWN_ROW_CHUNKS),
        grid=(rows // tm, nk, 2),
        in_specs=in_specs,
        out_specs=out_specs,
        out_shape=out_shape,
        scratch_shapes=[pltpu.VMEM((2, tm, tn), F32)],
        compiler_params=_params(("arbitrary", "arbitrary", "arbitrary")),
        name="mlp_down",
    )(*args)


def _epi_relu2(acc):
    a = jnp.maximum(acc, 0.0)
    return a * a


def _epi_plain(acc):
    return acc


def _rope(a, cos, sin_lo, sin_hi):
    up = pltpu.roll(a, HEAD_DIM - HEAD_DIM // 4, 1)
    dn = pltpu.roll(a, HEAD_DIM // 4, 1)
    return a * cos + up * sin_lo + dn * sin_hi


def _qkv_kernel(x_ref, w_ref, cq_ref, lq_ref, hq_ref, ck_ref, lk_ref, hk_ref, o_ref, wb_ref,
                *, n_cast, row_chunks):
    step = pl.program_id(0)
    kc = w_ref.shape[0]
    rc = x_ref.shape[0] // row_chunks

    @pl.when(step < n_cast)
    def _():
        wb_ref[pl.ds(pl.multiple_of(step * kc, kc), kc), :] = w_ref[...].astype(BF16)

    @pl.when(step >= n_cast)
    def _():
        for r in range(row_chunks):
            rows = pl.ds(r * rc, rc)
            acc = jnp.dot(x_ref[rows, :], wb_ref[...], preferred_element_type=F32)
            q_tabs = (cq_ref[rows, :], lq_ref[rows, :], hq_ref[rows, :])
            k_tabs = (ck_ref[rows, :], lk_ref[rows, :], hk_ref[rows, :])
            for hh in range(N_HEADS + N_KV_HEADS):
                cols = slice(hh * HEAD_DIM, (hh + 1) * HEAD_DIM)
                tabs = q_tabs if hh < N_HEADS else k_tabs
                o_ref[rows, cols] = _rope(acc[:, cols], *tabs).astype(o_ref.dtype)
            v_cols = slice((N_HEADS + N_KV_HEADS) * HEAD_DIM, QKV_W)
            o_ref[rows, v_cols] = acc[:, v_cols].astype(o_ref.dtype)


def _qkv_latent(u, w_qkv, tables):
    tm = QKV_TM
    n_cast = CAST_STEPS
    kc = D_MODEL // n_cast
    pos_blocks = SEQ // tm

    def tile(s):
        t = jnp.maximum(s - n_cast, 0)
        return (t % BATCH) * pos_blocks + t // BATCH

    def table_spec(group):
        return pl.BlockSpec((None, tm, HEAD_DIM), lambda s: (group, jnp.maximum(s - n_cast, 0) // BATCH, 0))

    cos2, lo2, hi2 = tables
    return pl.pallas_call(
        functools.partial(_qkv_kernel, n_cast=n_cast, row_chunks=ROW_CHUNKS),
        grid=(n_cast + T_LAT // tm,),
        in_specs=[
            pl.BlockSpec((tm, D_MODEL), lambda s: (tile(s), 0)),
            pl.BlockSpec((None, kc, QKV_W), lambda s: (0, jnp.minimum(s, n_cast - 1), 0)),
            table_spec(0), table_spec(0), table_spec(0),
            table_spec(1), table_spec(1), table_spec(1),
        ],
        out_specs=pl.BlockSpec((tm, QKV_W), lambda s: (tile(s), 0)),
        out_shape=jax.ShapeDtypeStruct((T_LAT, QKV_W), BF16),
        scratch_shapes=[pltpu.VMEM((D_MODEL, QKV_W), BF16)],
        compiler_params=_params(("arbitrary",)),
        name="qkv_latent",
    )(u, w_qkv, cos2, lo2, hi2, cos2, lo2, hi2)


def _proj_norm_kernel(*refs, n_src, n_cast, n_lat_tiles, row_chunks):
    x_ref, w_ref = refs[:2]
    h_refs = refs[2:2 + n_src]
    gate_ref, g_ref, sc_ref, sh_ref, h_out, u_out, wb_ref = refs[2 + n_src:]
    step = pl.program_id(0)
    kc = w_ref.shape[0]
    rc = x_ref.shape[0] // row_chunks

    @pl.when(step < n_cast)
    def _():
        wb_ref[pl.ds(pl.multiple_of(step * kc, kc), kc), :] = w_ref[...].astype(BF16)

    def compute(h_ref):
        for r in range(row_chunks):
            rows = pl.ds(r * rc, rc)
            acc = jnp.dot(x_ref[rows, :], wb_ref[...], preferred_element_type=F32)
            h_new = h_ref[rows, :] + gate_ref[...] * acc
            h_out[rows, :] = h_new
            u_out[rows, :] = _rms_mod(h_new, g_ref, sc_ref, sh_ref).astype(u_out.dtype)

    @pl.when(step >= n_cast)
    def _():
        _pick_stream(h_refs, step - n_cast, n_lat_tiles, compute)


def _proj_norm(x, w, layer, h_srcs, mod, g, rows):
    tm = PROJ_TM
    n_cast = CAST_STEPS
    kc = D_MODEL // n_cast

    def tile(s):
        return jnp.maximum(s - n_cast, 0)

    def mod_spec(blk):
        return pl.BlockSpec((None, 1, D_MODEL), lambda s: (_mod_row(tile(s), tm), 0, blk))

    body = functools.partial(_proj_norm_kernel, n_src=len(h_srcs), n_cast=n_cast,
                             n_lat_tiles=T_LAT // tm, row_chunks=ROW_CHUNKS)
    return pl.pallas_call(
        body,
        grid=(n_cast + rows // tm,),
        in_specs=[
            pl.BlockSpec((tm, D_MODEL), lambda s: (tile(s), 0)),
            pl.BlockSpec((None, kc, D_MODEL), lambda s: (layer, jnp.minimum(s, n_cast - 1), 0)),
        ] + _stream_specs(len(h_srcs), tm, tile) + [
            mod_spec(2),
            pl.BlockSpec((1, D_MODEL), lambda s: (0, 0)),
            mod_spec(4),
            mod_spec(3),
        ],
        out_specs=[pl.BlockSpec((tm, D_MODEL), lambda s: (tile(s), 0)),
                   pl.BlockSpec((tm, D_MODEL), lambda s: (tile(s), 0))],
        out_shape=[jax.ShapeDtypeStruct((rows, D_MODEL), F32),
                   jax.ShapeDtypeStruct((rows, D_MODEL), BF16)],
        scratch_shapes=[pltpu.VMEM((D_MODEL, D_MODEL), BF16)],
        compiler_params=_params(("arbitrary",)),
        name="proj_norm",
    )(x, w, *h_srcs, mod, g.reshape(1, D_MODEL), mod, mod)


def _conv_mixer_kernel(x_ref, wb_ref, wc_ref, wh_ref, cw_ref, o_ref, sb_ref, sc_ref, sh_ref, zk_ref, bk_ref,
                       *, row_chunks):
    i = pl.program_id(1)
    tm, tn = o_ref.shape
    rc = tm // row_chunks
    sub = F32_SUBLANES

    @pl.when(i == 0)
    def _():
        sb_ref[...] = wb_ref[...].astype(BF16)
        sc_ref[...] = wc_ref[...].astype(BF16)
        sh_ref[...] = wh_ref[...].astype(BF16)
        zk_ref[pl.ds(0, sub), :] = jnp.zeros((sub, tn), F32)
        zk_ref[pl.ds(sub + tm, sub), :] = jnp.zeros((sub, tn), F32)

    row0 = i * tm
    seq_len = jnp.where(row0 >= T_LAT, CTX_LEN, SEQ)

    def project(r):
        rows = pl.ds(r * rc, rc)
        x = x_ref[rows, :]
        bk_ref[rows, :] = jnp.dot(x, sb_ref[...], preferred_element_type=F32)
        zk_ref[pl.ds(sub + r * rc, rc), :] = (jnp.dot(x, sc_ref[...], preferred_element_type=F32)
                                               * jnp.dot(x, sh_ref[...], preferred_element_type=F32))

    def finish(r):
        base = r * rc
        pos = (row0 + base + lax.broadcasted_iota(jnp.int32, (rc, 1), 0)) & (seq_len - 1)
        zp = zk_ref[pl.ds(base, rc + 2 * sub), :]
        z_prev = jnp.where(pos == 0, 0.0, pltpu.roll(zp, 1, 0)[sub:sub + rc, :])
        z_next = jnp.where(pos == seq_len - 1, 0.0, pltpu.roll(zp, rc + 2 * sub - 1, 0)[sub:sub + rc, :])
        conv = (z_prev * cw_ref[pl.ds(0, 1), :] + zp[sub:sub + rc, :] * cw_ref[pl.ds(1, 1), :]
                + z_next * cw_ref[pl.ds(2, 1), :])
        o_ref[pl.ds(base, rc), :] = (bk_ref[pl.ds(base, rc), :] * conv).astype(o_ref.dtype)

    project(0)
    for r in range(1, row_chunks):
        project(r)
        finish(r - 1)
    finish(row_chunks - 1)


def _conv_mixer_in(u, w_in, conv_w, rows):
    tm, tn = CONV_TILE
    assert tm % CTX_LEN == 0 and T_LAT % tm == 0 and rows % tm == 0
    nblk = D_MODEL // tn

    def w_spec(part):
        return pl.BlockSpec((None, D_MODEL, tn), lambda j, i: (0, 0, j + part * nblk),
                            pipeline_mode=pl.Buffered(1))

    return pl.pallas_call(
        functools.partial(_conv_mixer_kernel, row_chunks=CONV_ROW_CHUNKS),
        grid=(nblk, rows // tm),
        in_specs=[
            pl.BlockSpec((tm, D_MODEL), lambda j, i: (i, 0)),
            w_spec(0), w_spec(1), w_spec(2),
            pl.BlockSpec((None, 3, tn), lambda j, i: (0, 0, j)),
        ],
        out_specs=pl.BlockSpec((tm, tn), lambda j, i: (i, j)),
        out_shape=jax.ShapeDtypeStruct((rows, D_MODEL), BF16),
        scratch_shapes=[pltpu.VMEM((D_MODEL, tn), BF16) for _ in range(3)] + [
            pltpu.VMEM((tm + 2 * F32_SUBLANES, tn), F32),
            pltpu.VMEM((tm, tn), F32),
        ],
        compiler_params=_params(("arbitrary", "arbitrary")),
        name="conv_mixer_in",
    )(u, w_in, w_in, w_in, conv_w)


def _attn_kernel(sink_ref, q_ref, k_ref, v_ref, kc_ref, vc_ref, o_ref, va_ref, vca_ref, kt_ref):
    kvh = pl.program_id(1)
    va_ref[:, :HEAD_DIM] = v_ref[...]
    va_ref[:, HEAD_DIM:] = jnp.ones((SEQ, HEAD_DIM), BF16)
    vca_ref[:, :HEAD_DIM] = vc_ref[...]
    vca_ref[:, HEAD_DIM:] = jnp.ones((CTX_LEN, HEAD_DIM), BF16)
    kt_ref[...] = k_ref[...].T
    kct = kc_ref[...].T
    vca = vca_ref[...]
    rel = (lax.broadcasted_iota(jnp.int32, (WINDOW, KWIN), 0)
           - lax.broadcasted_iota(jnp.int32, (WINDOW, KWIN), 1))
    masks = {}

    for n in range(SEQ // WINDOW):
        q0 = n * WINDOW
        start = min(max(q0 - WINDOW, 0), SEQ - KWIN)
        if q0 - start not in masks:
            masks[q0 - start] = jnp.abs(rel + (q0 - start)) <= WINDOW
        valid = masks[q0 - start]
        vwa = va_ref[pl.ds(start, KWIN), :]
        p_loc, p_ctx, sink_terms = [], [], []
        for g in range(GROUP):
            qg = q_ref[pl.ds(q0, WINDOW), g * HEAD_DIM:(g + 1) * HEAD_DIM]
            sl = jnp.where(valid, jnp.dot(qg, kt_ref[:, start:start + KWIN], preferred_element_type=F32), NEG)
            sc = jnp.dot(qg, kct, preferred_element_type=F32)
            sk = sink_ref[kvh * GROUP + g] * LOG2E
            m = jnp.maximum(jnp.max(jnp.concatenate([sl, sc], axis=1), axis=-1, keepdims=True), sk)
            p_loc.append(jnp.exp2(sl - m).astype(BF16))
            p_ctx.append(jnp.exp2(sc - m).astype(BF16))
            sink_terms.append(jnp.exp2(sk - m))
        oa = (jnp.dot(jnp.concatenate(p_loc, axis=0), vwa, preferred_element_type=F32)
              + jnp.dot(jnp.concatenate(p_ctx, axis=0), vca, preferred_element_type=F32))
        o = oa[:, :HEAD_DIM] / (oa[:, HEAD_DIM:] + jnp.concatenate(sink_terms, axis=0))
        for g in range(GROUP):
            o_ref[pl.ds(q0, WINDOW), g * HEAD_DIM:(g + 1) * HEAD_DIM] = (
                o[g * WINDOW:(g + 1) * WINDOW].astype(o_ref.dtype))


def _attention(qkv, kvc, sink):
    qw = GROUP * HEAD_DIM
    k_off = N_HEADS
    v_off = N_HEADS + N_KV_HEADS
    return pl.pallas_call(
        _attn_kernel,
        grid=(BATCH, N_KV_HEADS),
        in_specs=[
            pl.BlockSpec(memory_space=pltpu.SMEM),
            pl.BlockSpec((SEQ, qw), lambda b, h: (b, h)),
            pl.BlockSpec((SEQ, HEAD_DIM), lambda b, h: (b, k_off + h)),
            pl.BlockSpec((SEQ, HEAD_DIM), lambda b, h: (b, v_off + h)),
            pl.BlockSpec((CTX_LEN, HEAD_DIM), lambda b, h: (b, h)),
            pl.BlockSpec((CTX_LEN, HEAD_DIM), lambda b, h: (b, N_KV_HEADS + h)),
        ],
        out_specs=pl.BlockSpec((SEQ, qw), lambda b, h: (b, h)),
        out_shape=jax.ShapeDtypeStruct((T_LAT, N_HEADS * HEAD_DIM), BF16),
        scratch_shapes=[pltpu.VMEM((SEQ, 2 * HEAD_DIM), BF16), pltpu.VMEM((CTX_LEN, 2 * HEAD_DIM), BF16),
                        pltpu.VMEM((HEAD_DIM, SEQ), BF16)],
        compiler_params=_params(("parallel", "parallel")),
        name="window_attention",
    )(sink, qkv, qkv, qkv, kvc, kvc)


def _rope_tables():
    rows_n = SEQ // GRID_W
    row = jnp.repeat(jnp.arange(rows_n), GRID_W).astype(F32)
    col = jnp.tile(jnp.arange(GRID_W), rows_n).astype(F32)
    nf = HEAD_DIM // 4
    inv_freq = ROPE_BASE ** (-jnp.arange(nf, dtype=F32) / nf)
    ang_r = row[:, None] * inv_freq[None, :]
    ang_c = col[:, None] * inv_freq[None, :]
    ang = jnp.concatenate([ang_r, ang_r, ang_c, ang_c], axis=-1)
    cos, sin = jnp.cos(ang), jnp.sin(ang)
    low_half = (jnp.arange(HEAD_DIM) % (2 * nf)) < nf
    sin_lo = jnp.where(low_half[None, :], -sin, 0.0)
    sin_hi = jnp.where(low_half[None, :], 0.0, sin)
    scale = LOG2E / math.sqrt(HEAD_DIM)
    return (jnp.stack([cos * scale, cos]), jnp.stack([sin_lo * scale, sin_lo]),
            jnp.stack([sin_hi * scale, sin_hi]))


def _mlp(h, u2, mod, w1, w2, layer, g_next, mod_next, rows):
    tm, tn = MLP_UP_TILE
    a, w2_bf16 = _matmul_ws(u2, w1, layer, 0, _epi_relu2, rows=rows, tm=tm, tn=tn, n_cols=D_FF,
                            rider=(w2, layer, T_LAT // tm), name="mlp_up")
    return _mlp_down(a, w2_bf16, h, mod, g_next, mod_next, rows=rows)


def kernel(x, c, ctx, c_ctx, norm1_g, norm2_g, mod_w, mod_b, conv_w_in, conv_w, conv_w_out,
           attn_w_qkv, attn_sink, attn_w_o, mlp_w1, mlp_w2, final_g):
    x_lat = x.reshape(T_LAT, D_MODEL)
    x_ctx = ctx.reshape(T_CTX, D_MODEL)
    cvec = jnp.concatenate(
        [c, c_ctx[None, :], jnp.zeros((MOD_ROWS - BATCH - 1, D_MODEL), F32)], axis=0)
    mod = _mod_table(cvec, mod_w, mod_b)

    m0 = mod[0]
    u = _norm_mod([x_lat, x_ctx], norm1_g[0], m0, 0, 1, T_ALL)
    v = _conv_mixer_in(u, conv_w_in, conv_w, T_ALL)
    h, u2 = _proj_norm(v, conv_w_out, 0, [x_lat, x_ctx], m0, norm2_g[0], T_ALL)
    m1 = mod[1]
    h, u = _mlp(h, u2, m0, mlp_w1, mlp_w2, 0, norm1_g[1], m1, T_ALL)

    qkv = _qkv_latent(u, attn_w_qkv, _rope_tables())
    kv_cols = 2 * N_KV_HEADS * HEAD_DIM
    tm, tn = KV_CTX_TILE
    kvc = _matmul_ws(u, attn_w_qkv, 0, N_HEADS * HEAD_DIM // tn, _epi_plain, rows=T_CTX, tm=tm, tn=tn,
                     n_cols=kv_cols, row_blk_off=T_LAT // tm, name="kv_context")[0]
    o = _attention(qkv, kvc, attn_sink[0])
    h, u2 = _proj_norm(o, attn_w_o, 0, [h], m1, norm2_g[1], T_LAT)
    out = _mlp(h, u2, m1, mlp_w1, mlp_w2, 1, final_g, None, T_LAT)[0]
    return out.reshape(BATCH, SEQ, D_MODEL)
```

```python
import functools
import math

import jax
import jax.numpy as jnp
from jax import lax
from jax.experimental import pallas as pl
from jax.experimental.pallas import tpu as pltpu

F32 = jnp.float32
BF16 = jnp.bfloat16

D_MODEL = 2048
BATCH = 8
SEQ = 2048
CTX_LEN = 256
GRID_W = 64
HEAD_DIM = 128
N_HEADS = 16
N_KV_HEADS = 4
GROUP = 4
WINDOW = 128
ROPE_BASE = 10000.0
D_FF = 4 * D_MODEL
N_MOD = 6
EPS = 1e-6
NEG = -1e30
LOG2E = math.log2(math.e)

T_LAT = BATCH * SEQ
T_CTX = BATCH * CTX_LEN
T_ALL = T_LAT + T_CTX
MOD_ROWS = 16
QKV_W = (N_HEADS + 2 * N_KV_HEADS) * HEAD_DIM
KWIN = 3 * WINDOW

F32_SUBLANES = 8
VMEM_LIMIT = 60 * 1024 * 1024

MOD_TN = 1024
NORM_TM = 1024
NORM_PASS_ROWS = 16
CONV_TILE = (SEQ, 512)
CONV_ROW_CHUNKS = 4
PROJ_TM = 512
QKV_TM = 512
CAST_STEPS = 4
ROW_CHUNKS = 2
MLP_UP_TILE = (2048, 1024)
MLP_DOWN_TILE = (1024, 2048)
MLP_DOWN_ROW_CHUNKS = 4
KV_CTX_TILE = (1024, 1024)


def _params(sem):
    return pltpu.CompilerParams(dimension_semantics=sem, vmem_limit_bytes=VMEM_LIMIT)


def _mod_row(i, tm):
    return jnp.minimum((i * tm) // SEQ, BATCH)


def _mod_kernel(c_ref, w_ref, b_ref, o_ref):
    s = jax.nn.silu(c_ref[...]).astype(BF16)
    o_ref[...] = jnp.dot(s, w_ref[...].astype(BF16), preferred_element_type=F32) + b_ref[...]


def _mod_table(cvec, mod_w, mod_b):
    depth = mod_w.shape[0]
    n = N_MOD * D_MODEL
    tn = MOD_TN
    out = pl.pallas_call(
        _mod_kernel,
        grid=(depth, n // tn),
        in_specs=[
            pl.BlockSpec((MOD_ROWS, D_MODEL), lambda l, j: (0, 0)),
            pl.BlockSpec((None, D_MODEL, tn), lambda l, j: (l, 0, j)),
            pl.BlockSpec((None, 1, tn), lambda l, j: (l, 0, j)),
        ],
        out_specs=pl.BlockSpec((None, MOD_ROWS, tn), lambda l, j: (l, 0, j)),
        out_shape=jax.ShapeDtypeStruct((depth, MOD_ROWS, n), F32),
        compiler_params=_params(("parallel", "parallel")),
        name="mod_table",
    )(cvec, mod_w, mod_b.reshape(depth, 1, n))
    return out.reshape(depth, MOD_ROWS, 1, n)


def _rms_mod(x, g_ref, sc_ref, sh_ref):
    y = x * lax.rsqrt(jnp.mean(x * x, axis=-1, keepdims=True) + EPS)
    y = y * g_ref[...]
    return y * (1.0 + sc_ref[...]) + sh_ref[...]


def _pick_stream(refs, tile, n_lat_tiles, fn):
    if len(refs) == 1:
        fn(refs[0])
        return
    pl.when(tile < n_lat_tiles)(lambda: fn(refs[0]))
    pl.when(tile >= n_lat_tiles)(lambda: fn(refs[1]))


def _stream_specs(n_src, tm, tile_of):
    n_lat = T_LAT // tm
    if n_src == 1:
        return [pl.BlockSpec((tm, D_MODEL), lambda *g: (tile_of(*g), 0))]
    return [
        pl.BlockSpec((tm, D_MODEL), lambda *g: (jnp.minimum(tile_of(*g), n_lat - 1), 0)),
        pl.BlockSpec((tm, D_MODEL), lambda *g: (jnp.maximum(tile_of(*g) - n_lat, 0), 0)),
    ]


def _norm_mod_kernel(*refs, n_lat_tiles):
    srcs = refs[:-4]
    g_ref, sc_ref, sh_ref, o_ref = refs[-4:]

    def emit(x_ref):
        rc = NORM_PASS_ROWS
        for r in range(o_ref.shape[0] // rc):
            rows = pl.ds(r * rc, rc)
            o_ref[rows, :] = _rms_mod(x_ref[rows, :], g_ref, sc_ref, sh_ref).astype(o_ref.dtype)

    _pick_stream(srcs, pl.program_id(0), n_lat_tiles, emit)


def _norm_mod(srcs, g, mod, shift_blk, scale_blk, rows):
    tm = NORM_TM
    return pl.pallas_call(
        functools.partial(_norm_mod_kernel, n_lat_tiles=T_LAT // tm),
        grid=(rows // tm,),
        in_specs=_stream_specs(len(srcs), tm, lambda i: i) + [
            pl.BlockSpec((1, D_MODEL), lambda i: (0, 0)),
            pl.BlockSpec((None, 1, D_MODEL), lambda i: (_mod_row(i, tm), 0, scale_blk)),
            pl.BlockSpec((None, 1, D_MODEL), lambda i: (_mod_row(i, tm), 0, shift_blk)),
        ],
        out_specs=pl.BlockSpec((tm, D_MODEL), lambda i: (i, 0)),
        out_shape=jax.ShapeDtypeStruct((rows, D_MODEL), BF16),
        compiler_params=_params(("parallel",)),
        name="norm_mod",
    )(*srcs, g.reshape(1, D_MODEL), mod, mod)


def _ws_kernel(*refs, has_rider, epilogue):
    if has_rider:
        x_ref, w_ref, rider_in, o_ref, rider_out, strip_ref = refs
        rider_out[...] = rider_in[...].astype(BF16)
    else:
        x_ref, w_ref, o_ref, strip_ref = refs

    @pl.when(pl.program_id(1) == 0)
    def _():
        strip_ref[...] = w_ref[...].astype(BF16)

    acc = jnp.dot(x_ref[...], strip_ref[...], preferred_element_type=F32)
    o_ref[...] = epilogue(acc).astype(o_ref.dtype)


def _matmul_ws(x, w, layer, col_blk_off, epilogue, *, rows, tm, tn, n_cols, row_blk_off=0, rider=None, name):
    kdim = x.shape[1]
    n_j, n_i = n_cols // tn, rows // tm
    in_specs = [
        pl.BlockSpec((tm, kdim), lambda j, i: (i + row_blk_off, 0)),
        pl.BlockSpec((None, kdim, tn), lambda j, i: (layer, 0, j + col_blk_off)),
    ]
    args = [x, w]
    out_specs = [pl.BlockSpec((tm, tn), lambda j, i: (i, j))]
    out_shape = [jax.ShapeDtypeStruct((rows, n_cols), BF16)]
    if rider is not None:
        rider_w, rider_layer, per_j = rider
        _, rider_rows, rider_cols = rider_w.shape
        rb = rider_rows // (n_j * per_j)

        def rider_blk(j, i):
            return j * per_j + jnp.minimum(i, per_j - 1)

        in_specs.append(pl.BlockSpec((None, rb, rider_cols), lambda j, i: (rider_layer, rider_blk(j, i), 0)))
        args.append(rider_w)
        out_specs.append(pl.BlockSpec((rb, rider_cols), lambda j, i: (rider_blk(j, i), 0)))
        out_shape.append(jax.ShapeDtypeStruct((rider_rows, rider_cols), BF16))
    return pl.pallas_call(
        functools.partial(_ws_kernel, has_rider=rider is not None, epilogue=epilogue),
        grid=(n_j, n_i),
        in_specs=in_specs,
        out_specs=out_specs,
        out_shape=out_shape,
        scratch_shapes=[pltpu.VMEM((kdim, tn), BF16)],
        compiler_params=_params(("arbitrary", "arbitrary")),
        name=name,
    )(*args)


def _mlp_down_kernel(*refs, nk, final, row_chunks):
    if final:
        x_ref, w_ref, h_ref, gate_ref, g_ref, out_ref, acc_ref = refs
        sc_ref = sh_ref = h_out = None
    else:
        x_ref, w_ref, h_ref, gate_ref, g_ref, sc_ref, sh_ref, h_out, out_ref, acc_ref = refs
    k = pl.program_id(1)
    j = pl.program_id(2)
    _, tm, tn = acc_ref.shape
    rc = tm // row_chunks

    @pl.when(k == 0)
    def _():
        acc_ref[j] = jnp.dot(x_ref[...], w_ref[...], preferred_element_type=F32)

    @pl.when(jnp.logical_and(k > 0, k < nk - 1))
    def _():
        acc_ref[j] += jnp.dot(x_ref[...], w_ref[...], preferred_element_type=F32)

    def h_new_rows(half, rows):
        acc = acc_ref[half, rows, :] + jnp.dot(x_ref[rows, :], w_ref[...], preferred_element_type=F32)
        h_new = h_ref[rows, :] + gate_ref[...] * acc
        if not final:
            h_out[rows, :] = h_new
        return h_new

    @pl.when(jnp.logical_and(k == nk - 1, j == 0))
    def _():
        for r in range(row_chunks):
            rows = pl.ds(r * rc, rc)
            acc_ref[0, rows, :] = h_new_rows(0, rows)

    @pl.when(jnp.logical_and(k == nk - 1, j == 1))
    def _():
        for r in range(row_chunks):
            rows = pl.ds(r * rc, rc)
            halves = (acc_ref[0, rows, :], h_new_rows(1, rows))
            ssq = sum(jnp.sum(hv * hv, axis=-1, keepdims=True) for hv in halves)
            rstd = lax.rsqrt(ssq * (1.0 / D_MODEL) + EPS)
            for half, hv in enumerate(halves):
                cols = slice(half * tn, (half + 1) * tn)
                y = hv * rstd * g_ref[:, cols]
                if not final:
                    y = y * (1.0 + sc_ref[:, cols]) + sh_ref[:, cols]
                out_ref[rows, cols] = y.astype(out_ref.dtype)


def _mlp_down(x, w, h, mod, g, mod_next, *, rows):
    tm, tk = MLP_DOWN_TILE
    tn = D_MODEL // 2
    nk = D_FF // tk
    final = mod_next is None

    def half_of(k, j):
        return jnp.where(k == nk - 1, j, 0)

    def mod_spec(blk, width):
        return pl.BlockSpec((None, 1, width), lambda i, k, j: (_mod_row(i, tm), 0, blk))

    in_specs = [
        pl.BlockSpec((tm, tk), lambda i, k, j: (i, k)),
        pl.BlockSpec((tk, tn), lambda i, k, j: (k, j)),
        pl.BlockSpec((tm, tn), lambda i, k, j: (i, half_of(k, j))),
        pl.BlockSpec((None, 1, tn), lambda i, k, j: (_mod_row(i, tm), 0, 5 * 2 + half_of(k, j))),
        pl.BlockSpec((1, D_MODEL), lambda i, k, j: (0, 0)),
    ]
    args = [x, w, h, mod, g.reshape(1, D_MODEL)]
    full_rows = pl.BlockSpec((tm, D_MODEL), lambda i, k, j: (i, 0))
    if final:
        out_specs = [full_rows]
        out_shape = [jax.ShapeDtypeStruct((rows, D_MODEL), F32)]
    else:
        in_specs += [mod_spec(1, D_MODEL), mod_spec(0, D_MODEL)]
        args += [mod_next, mod_next]
        out_specs = [pl.BlockSpec((tm, tn), lambda i, k, j: (i, half_of(k, j))), full_rows]
        out_shape = [jax.ShapeDtypeStruct((rows, D_MODEL), F32), jax.ShapeDtypeStruct((rows, D_MODEL), BF16)]
    return pl.pallas_call(
        functools.partial(_mlp_down_kernel, nk=nk, final=final, row_chunks=MLP_DOWN_ROW_CHUNKS),
        grid=(rows // tm, nk, 2),
        in_specs=in_specs,
        out_specs=out_specs,
        out_shape=out_shape,
        scratch_shapes=[pltpu.VMEM((2, tm, tn), F32)],
        compiler_params=_params(("arbitrary", "arbitrary", "arbitrary")),
        name="mlp_down",
    )(*args)


def _epi_relu2(acc):
    a = jnp.maximum(acc, 0.0)
    return a * a


def _epi_plain(acc):
    return acc


def _rope(a, cos, sin_lo, sin_hi):
    up = pltpu.roll(a, HEAD_DIM - HEAD_DIM // 4, 1)
    dn = pltpu.roll(a, HEAD_DIM // 4, 1)
    return a * cos + up * sin_lo + dn * sin_hi


def _qkv_kernel(x_ref, w_ref, cq_ref, lq_ref, hq_ref, ck_ref, lk_ref, hk_ref, o_ref, wb_ref,
                *, n_cast, row_chunks):
    step = pl.program_id(0)
    kc = w_ref.shape[0]
    rc = x_ref.shape[0] // row_chunks

    @pl.when(step < n_cast)
    def _():
        wb_ref[pl.ds(pl.multiple_of(step * kc, kc), kc), :] = w_ref[...].astype(BF16)

    @pl.when(step >= n_cast)
    def _():
        for r in range(row_chunks):
            rows = pl.ds(r * rc, rc)
            acc = jnp.dot(x_ref[rows, :], wb_ref[...], preferred_element_type=F32)
            q_tabs = (cq_ref[rows, :], lq_ref[rows, :], hq_ref[rows, :])
            k_tabs = (ck_ref[rows, :], lk_ref[rows, :], hk_ref[rows, :])
            for hh in range(N_HEADS + N_KV_HEADS):
                cols = slice(hh * HEAD_DIM, (hh + 1) * HEAD_DIM)
                tabs = q_tabs if hh < N_HEADS else k_tabs
                o_ref[rows, cols] = _rope(acc[:, cols], *tabs).astype(o_ref.dtype)
            v_cols = slice((N_HEADS + N_KV_HEADS) * HEAD_DIM, QKV_W)
            o_ref[rows, v_cols] = acc[:, v_cols].astype(o_ref.dtype)


def _qkv_latent(u, w_qkv, tables):
    tm = QKV_TM
    n_cast = CAST_STEPS
    kc = D_MODEL // n_cast
    pos_blocks = SEQ // tm

    def tile(s):
        t = jnp.maximum(s - n_cast, 0)
        return (t % BATCH) * pos_blocks + t // BATCH

    def table_spec(group):
        return pl.BlockSpec((None, tm, HEAD_DIM), lambda s: (group, jnp.maximum(s - n_cast, 0) // BATCH, 0))

    cos2, lo2, hi2 = tables
    return pl.pallas_call(
        functools.partial(_qkv_kernel, n_cast=n_cast, row_chunks=ROW_CHUNKS),
        grid=(n_cast + T_LAT // tm,),
        in_specs=[
            pl.BlockSpec((tm, D_MODEL), lambda s: (tile(s), 0)),
            pl.BlockSpec((None, kc, QKV_W), lambda s: (0, jnp.minimum(s, n_cast - 1), 0)),
            table_spec(0), table_spec(0), table_spec(0),
            table_spec(1), table_spec(1), table_spec(1),
        ],
        out_specs=pl.BlockSpec((tm, QKV_W), lambda s: (tile(s), 0)),
        out_shape=jax.ShapeDtypeStruct((T_LAT, QKV_W), BF16),
        scratch_shapes=[pltpu.VMEM((D_MODEL, QKV_W), BF16)],
        compiler_params=_params(("arbitrary",)),
        name="qkv_latent",
    )(u, w_qkv, cos2, lo2, hi2, cos2, lo2, hi2)


def _proj_norm_kernel(*refs, n_src, n_cast, n_lat_tiles, row_chunks):
    x_ref, w_ref = refs[:2]
    h_refs = refs[2:2 + n_src]
    gate_ref, g_ref, sc_ref, sh_ref, h_out, u_out, wb_ref = refs[2 + n_src:]
    step = pl.program_id(0)
    kc = w_ref.shape[0]
    rc = x_ref.shape[0] // row_chunks

    @pl.when(step < n_cast)
    def _():
        wb_ref[pl.ds(pl.multiple_of(step * kc, kc), kc), :] = w_ref[...].astype(BF16)

    def compute(h_ref):
        for r in range(row_chunks):
            rows = pl.ds(r * rc, rc)
            acc = jnp.dot(x_ref[rows, :], wb_ref[...], preferred_element_type=F32)
            h_new = h_ref[rows, :] + gate_ref[...] * acc
            h_out[rows, :] = h_new
            u_out[rows, :] = _rms_mod(h_new, g_ref, sc_ref, sh_ref).astype(u_out.dtype)

    @pl.when(step >= n_cast)
    def _():
        _pick_stream(h_refs, step - n_cast, n_lat_tiles, compute)


def _proj_norm(x, w, layer, h_srcs, mod, g, rows):
    tm = PROJ_TM
    n_cast = CAST_STEPS
    kc = D_MODEL // n_cast

    def tile(s):
        return jnp.maximum(s - n_cast, 0)

    def mod_spec(blk):
        return pl.BlockSpec((None, 1, D_MODEL), lambda s: (_mod_row(tile(s), tm), 0, blk))

    body = functools.partial(_proj_norm_kernel, n_src=len(h_srcs), n_cast=n_cast,
                             n_lat_tiles=T_LAT // tm, row_chunks=ROW_CHUNKS)
    return pl.pallas_call(
        body,
        grid=(n_cast + rows // tm,),
        in_specs=[
            pl.BlockSpec((tm, D_MODEL), lambda s: (tile(s), 0)),
            pl.BlockSpec((None, kc, D_MODEL), lambda s: (layer, jnp.minimum(s, n_cast - 1), 0)),
        ] + _stream_specs(len(h_srcs), tm, tile) + [
            mod_spec(2),
            pl.BlockSpec((1, D_MODEL), lambda s: (0, 0)),
            mod_spec(4),
            mod_spec(3),
        ],
        out_specs=[pl.BlockSpec((tm, D_MODEL), lambda s: (tile(s), 0)),
                   pl.BlockSpec((tm, D_MODEL), lambda s: (tile(s), 0))],
        out_shape=[jax.ShapeDtypeStruct((rows, D_MODEL), F32),
                   jax.ShapeDtypeStruct((rows, D_MODEL), BF16)],
        scratch_shapes=[pltpu.VMEM((D_MODEL, D_MODEL), BF16)],
        compiler_params=_params(("arbitrary",)),
        name="proj_norm",
    )(x, w, *h_srcs, mod, g.reshape(1, D_MODEL), mod, mod)


def _conv_mixer_kernel(x_ref, wb_ref, wc_ref, wh_ref, cw_ref, o_ref, sb_ref, sc_ref, sh_ref, zk_ref, bk_ref,
                       *, row_chunks):
    i = pl.program_id(1)
    tm, tn = o_ref.shape
    rc = tm // row_chunks
    sub = F32_SUBLANES

    @pl.when(i == 0)
    def _():
        sb_ref[...] = wb_ref[...].astype(BF16)
        sc_ref[...] = wc_ref[...].astype(BF16)
        sh_ref[...] = wh_ref[...].astype(BF16)
        zk_ref[pl.ds(0, sub), :] = jnp.zeros((sub, tn), F32)
        zk_ref[pl.ds(sub + tm, sub), :] = jnp.zeros((sub, tn), F32)

    row0 = i * tm
    seq_len = jnp.where(row0 >= T_LAT, CTX_LEN, SEQ)

    def project(r):
        rows = pl.ds(r * rc, rc)
        x = x_ref[rows, :]
        bk_ref[rows, :] = jnp.dot(x, sb_ref[...], preferred_element_type=F32)
        zk_ref[pl.ds(sub + r * rc, rc), :] = (jnp.dot(x, sc_ref[...], preferred_element_type=F32)
                                               * jnp.dot(x, sh_ref[...], preferred_element_type=F32))

    def finish(r):
        base = r * rc
        pos = (row0 + base + lax.broadcasted_iota(jnp.int32, (rc, 1), 0)) & (seq_len - 1)
        zp = zk_ref[pl.ds(base, rc + 2 * sub), :]
        z_prev = jnp.where(pos == 0, 0.0, pltpu.roll(zp, 1, 0)[sub:sub + rc, :])
        z_next = jnp.where(pos == seq_len - 1, 0.0, pltpu.roll(zp, rc + 2 * sub - 1, 0)[sub:sub + rc, :])
        conv = (z_prev * cw_ref[pl.ds(0, 1), :] + zp[sub:sub + rc, :] * cw_ref[pl.ds(1, 1), :]
                + z_next * cw_ref[pl.ds(2, 1), :])
        o_ref[pl.ds(base, rc), :] = (bk_ref[pl.ds(base, rc), :] * conv).astype(o_ref.dtype)

    project(0)
    for r in range(1, row_chunks):
        project(r)
        finish(r - 1)
    finish(row_chunks - 1)


def _conv_mixer_in(u, w_in, conv_w, rows):
    tm, tn = CONV_TILE
    assert tm % CTX_LEN == 0 and T_LAT % tm == 0 and rows % tm == 0
    nblk = D_MODEL // tn

    def w_spec(part):
        return pl.BlockSpec((None, D_MODEL, tn), lambda j, i: (0, 0, j + part * nblk),
                            pipeline_mode=pl.Buffered(1))

    return pl.pallas_call(
        functools.partial(_conv_mixer_kernel, row_chunks=CONV_ROW_CHUNKS),
        grid=(nblk, rows // tm),
        in_specs=[
            pl.BlockSpec((tm, D_MODEL), lambda j, i: (i, 0)),
            w_spec(0), w_spec(1), w_spec(2),
            pl.BlockSpec((None, 3, tn), lambda j, i: (0, 0, j)),
        ],
        out_specs=pl.BlockSpec((tm, tn), lambda j, i: (i, j)),
        out_shape=jax.ShapeDtypeStruct((rows, D_MODEL), BF16),
        scratch_shapes=[pltpu.VMEM((D_MODEL, tn), BF16) for _ in range(3)] + [
            pltpu.VMEM((tm + 2 * F32_SUBLANES, tn), F32),
            pltpu.VMEM((tm, tn), F32),
        ],
        compiler_params=_params(("arbitrary", "arbitrary")),
        name="conv_mixer_in",
    )(u, w_in, w_in, w_in, conv_w)


def _attn_kernel(sink_ref, q_ref, k_ref, v_ref, kc_ref, vc_ref, o_ref, va_ref, vca_ref, kt_ref):
    kvh = pl.program_id(1)
    va_ref[:, :HEAD_DIM] = v_ref[...]
    va_ref[:, HEAD_DIM:] = jnp.ones((SEQ, HEAD_DIM), BF16)
    vca_ref[:, :HEAD_DIM] = vc_ref[...]
    vca_ref[:, HEAD_DIM:] = jnp.ones((CTX_LEN, HEAD_DIM), BF16)
    kt_ref[...] = k_ref[...].T
    kct = kc_ref[...].T
    vca = vca_ref[...]
    rel = (lax.broadcasted_iota(jnp.int32, (WINDOW, KWIN), 0)
           - lax.broadcasted_iota(jnp.int32, (WINDOW, KWIN), 1))
    masks = {}

    for n in range(SEQ // WINDOW):
        q0 = n * WINDOW
        start = min(max(q0 - WINDOW, 0), SEQ - KWIN)
        if q0 - start not in masks:
            masks[q0 - start] = jnp.abs(rel + (q0 - start)) <= WINDOW
        valid = masks[q0 - start]
        vwa = va_ref[pl.ds(start, KWIN), :]
        p_loc, p_ctx, sink_terms = [], [], []
        for g in range(GROUP):
            qg = q_ref[pl.ds(q0, WINDOW), g * HEAD_DIM:(g + 1) * HEAD_DIM]
            sl = jnp.where(valid, jnp.dot(qg, kt_ref[:, start:start + KWIN], preferred_element_type=F32), NEG)
            sc = jnp.dot(qg, kct, preferred_element_type=F32)
            sk = sink_ref[kvh * GROUP + g] * LOG2E
            m = jnp.maximum(jnp.max(jnp.concatenate([sl, sc], axis=1), axis=-1, keepdims=True), sk)
            p_loc.append(jnp.exp2(sl - m).astype(BF16))
            p_ctx.append(jnp.exp2(sc - m).astype(BF16))
            sink_terms.append(jnp.exp2(sk - m))
        oa = (jnp.dot(jnp.concatenate(p_loc, axis=0), vwa, preferred_element_type=F32)
              + jnp.dot(jnp.concatenate(p_ctx, axis=0), vca, preferred_element_type=F32))
        o = oa[:, :HEAD_DIM] / (oa[:, HEAD_DIM:] + jnp.concatenate(sink_terms, axis=0))
        for g in range(GROUP):
            o_ref[pl.ds(q0, WINDOW), g * HEAD_DIM:(g + 1) * HEAD_DIM] = (
                o[g * WINDOW:(g + 1) * WINDOW].astype(o_ref.dtype))


def _attention(qkv, kvc, sink):
    qw = GROUP * HEAD_DIM
    k_off = N_HEADS
    v_off = N_HEADS + N_KV_HEADS
    return pl.pallas_call(
        _attn_kernel,
        grid=(BATCH, N_KV_HEADS),
        in_specs=[
            pl.BlockSpec(memory_space=pltpu.SMEM),
            pl.BlockSpec((SEQ, qw), lambda b, h: (b, h)),
            pl.BlockSpec((SEQ, HEAD_DIM), lambda b, h: (b, k_off + h)),
            pl.BlockSpec((SEQ, HEAD_DIM), lambda b, h: (b, v_off + h)),
            pl.BlockSpec((CTX_LEN, HEAD_DIM), lambda b, h: (b, h)),
            pl.BlockSpec((CTX_LEN, HEAD_DIM), lambda b, h: (b, N_KV_HEADS + h)),
        ],
        out_specs=pl.BlockSpec((SEQ, qw), lambda b, h: (b, h)),
        out_shape=jax.ShapeDtypeStruct((T_LAT, N_HEADS * HEAD_DIM), BF16),
        scratch_shapes=[pltpu.VMEM((SEQ, 2 * HEAD_DIM), BF16), pltpu.VMEM((CTX_LEN, 2 * HEAD_DIM), BF16),
                        pltpu.VMEM((HEAD_DIM, SEQ), BF16)],
        compiler_params=_params(("parallel", "parallel")),
        name="window_attention",
    )(sink, qkv, qkv, qkv, kvc, kvc)


def _rope_tables():
    rows_n = SEQ // GRID_W
    row = jnp.repeat(jnp.arange(rows_n), GRID_W).astype(F32)
    col = jnp.tile(jnp.arange(GRID_W), rows_n).astype(F32)
    nf = HEAD_DIM // 4
    inv_freq = ROPE_BASE ** (-jnp.arange(nf, dtype=F32) / nf)
    ang_r = row[:, None] * inv_freq[None, :]
    ang_c = col[:, None] * inv_freq[None, :]
    ang = jnp.concatenate([ang_r, ang_r, ang_c, ang_c], axis=-1)
    cos, sin = jnp.cos(ang), jnp.sin(ang)
    low_half = (jnp.arange(HEAD_DIM) % (2 * nf)) < nf
    sin_lo = jnp.where(low_half[None, :], -sin, 0.0)
    sin_hi = jnp.where(low_half[None, :], 0.0, sin)
    scale = LOG2E / math.sqrt(HEAD_DIM)
    return (jnp.stack([cos * scale, cos]), jnp.stack([sin_lo * scale, sin_lo]),
            jnp.stack([sin_hi * scale, sin_hi]))


def _mlp(h, u2, mod, w1, w2, layer, g_next, mod_next, rows):
    tm, tn = MLP_UP_TILE
    a, w2_bf16 = _matmul_ws(u2, w1, layer, 0, _epi_relu2, rows=rows, tm=tm, tn=tn, n_cols=D_FF,
                            rider=(w2, layer, T_LAT // tm), name="mlp_up")
    return _mlp_down(a, w2_bf16, h, mod, g_next, mod_next, rows=rows)


def kernel(x, c, ctx, c_ctx, norm1_g, norm2_g, mod_w, mod_b, conv_w_in, conv_w, conv_w_out,
           attn_w_qkv, attn_sink, attn_w_o, mlp_w1, mlp_w2, final_g):
    x_lat = x.reshape(T_LAT, D_MODEL)
    x_ctx = ctx.reshape(T_CTX, D_MODEL)
    cvec = jnp.concatenate(
        [c, c_ctx[None, :], jnp.zeros((MOD_ROWS - BATCH - 1, D_MODEL), F32)], axis=0)
    mod = _mod_table(cvec, mod_w, mod_b)

    m0 = mod[0]
    u = _norm_mod([x_lat, x_ctx], norm1_g[0], m0, 0, 1, T_ALL)
    v = _conv_mixer_in(u, conv_w_in, conv_w, T_ALL)
    h, u2 = _proj_norm(v, conv_w_out, 0, [x_lat, x_ctx], m0, norm2_g[0], T_ALL)
    m1 = mod[1]
    h, u = _mlp(h, u2, m0, mlp_w1, mlp_w2, 0, norm1_g[1], m1, T_ALL)

    qkv = _qkv_latent(u, attn_w_qkv, _rope_tables())
    kv_cols = 2 * N_KV_HEADS * HEAD_DIM
    tm, tn = KV_CTX_TILE
    kvc = _matmul_ws(u, attn_w_qkv, 0, N_HEADS * HEAD_DIM // tn, _epi_plain, rows=T_CTX, tm=tm, tn=tn,
                     n_cols=kv_cols, row_blk_off=T_LAT // tm, name="kv_context")[0]
    o = _attention(qkv, kvc, attn_sink[0])
    h, u2 = _proj_norm(o, attn_w_o, 0, [h], m1, norm2_g[1], T_LAT)
    out = _mlp(h, u2, m1, mlp_w1, mlp_w2, 1, final_g, None, T_LAT)[0]
    return out.reshape(BATCH, SEQ, D_MODEL)
```

```python
import functools
import math

import jax
import jax.numpy as jnp
from jax import lax
from jax.experimental import pallas as pl
from jax.experimental.pallas import tpu as pltpu

F32 = jnp.float32
BF16 = jnp.bfloat16

D_MODEL = 2048
BATCH = 8
SEQ = 2048
CTX_LEN = 256
GRID_W = 64
HEAD_DIM = 128
N_HEADS = 16
N_KV_HEADS = 4
GROUP = 4
WINDOW = 128
ROPE_BASE = 10000.0
D_FF = 4 * D_MODEL
N_MOD = 6
EPS = 1e-6
NEG = -1e30
LOG2E = math.log2(math.e)

T_LAT = BATCH * SEQ
T_CTX = BATCH * CTX_LEN
T_ALL = T_LAT + T_CTX
MOD_ROWS = 16
QKV_W = (N_HEADS + 2 * N_KV_HEADS) * HEAD_DIM
KWIN = 3 * WINDOW

F32_SUBLANES = 8
VMEM_LIMIT = 60 * 1024 * 1024

MOD_TN = 2048
NORM_TM = 1024
NORM_PASS_ROWS = 64
CONV_TILE = (SEQ, 512)
CONV_ROW_CHUNKS = 4
PROJ_TM = 512
QKV_TM = 512
CAST_STEPS = 4
ROW_CHUNKS = 2
MLP_UP_TILE = (2048, 1024)
MLP_DOWN_TILE = (1024, 2048)
MLP_DOWN_ROW_CHUNKS = 4


def _params(sem):
    return pltpu.CompilerParams(dimension_semantics=sem, vmem_limit_bytes=VMEM_LIMIT)


def _mod_row(i, tm):
    return jnp.minimum((i * tm) // SEQ, BATCH)


def _mod_kernel(c_ref, w_ref, b_ref, o_ref):
    s = jax.nn.silu(c_ref[...]).astype(BF16)
    o_ref[...] = jnp.dot(s, w_ref[...].astype(BF16), preferred_element_type=F32) + b_ref[...]


def _mod_table(cvec, mod_w, mod_b):
    depth = mod_w.shape[0]
    n = N_MOD * D_MODEL
    tn = MOD_TN
    out = pl.pallas_call(
        _mod_kernel,
        grid=(depth, n // tn),
        in_specs=[
            pl.BlockSpec((MOD_ROWS, D_MODEL), lambda l, j: (0, 0)),
            pl.BlockSpec((None, D_MODEL, tn), lambda l, j: (l, 0, j)),
            pl.BlockSpec((None, 1, tn), lambda l, j: (l, 0, j)),
        ],
        out_specs=pl.BlockSpec((None, MOD_ROWS, tn), lambda l, j: (l, 0, j)),
        out_shape=jax.ShapeDtypeStruct((depth, MOD_ROWS, n), F32),
        compiler_params=_params(("parallel", "parallel")),
        name="mod_table",
    )(cvec, mod_w, mod_b.reshape(depth, 1, n))
    return out.reshape(depth, MOD_ROWS, 1, n)


def _rms_mod(x, g_ref, sc_ref, sh_ref):
    y = x * lax.rsqrt(jnp.mean(x * x, axis=-1, keepdims=True) + EPS)
    y = y * g_ref[...]
    return y * (1.0 + sc_ref[...]) + sh_ref[...]


def _pick_stream(refs, tile, n_lat_tiles, fn):
    if len(refs) == 1:
        fn(refs[0])
        return
    pl.when(tile < n_lat_tiles)(lambda: fn(refs[0]))
    pl.when(tile >= n_lat_tiles)(lambda: fn(refs[1]))


def _stream_specs(n_src, tm, tile_of):
    n_lat = T_LAT // tm
    if n_src == 1:
        return [pl.BlockSpec((tm, D_MODEL), lambda *g: (tile_of(*g), 0))]
    return [
        pl.BlockSpec((tm, D_MODEL), lambda *g: (jnp.minimum(tile_of(*g), n_lat - 1), 0)),
        pl.BlockSpec((tm, D_MODEL), lambda *g: (jnp.maximum(tile_of(*g) - n_lat, 0), 0)),
    ]


def _norm_mod_kernel(*refs, n_lat_tiles):
    srcs = refs[:-4]
    g_ref, sc_ref, sh_ref, o_ref = refs[-4:]

    def emit(x_ref):
        rc = NORM_PASS_ROWS
        for r in range(o_ref.shape[0] // rc):
            rows = pl.ds(r * rc, rc)
            o_ref[rows, :] = _rms_mod(x_ref[rows, :], g_ref, sc_ref, sh_ref).astype(o_ref.dtype)

    _pick_stream(srcs, pl.program_id(0), n_lat_tiles, emit)


def _norm_mod(srcs, g, mod, shift_blk, scale_blk, rows):
    tm = NORM_TM
    return pl.pallas_call(
        functools.partial(_norm_mod_kernel, n_lat_tiles=T_LAT // tm),
        grid=(rows // tm,),
        in_specs=_stream_specs(len(srcs), tm, lambda i: i) + [
            pl.BlockSpec((1, D_MODEL), lambda i: (0, 0)),
            pl.BlockSpec((None, 1, D_MODEL), lambda i: (_mod_row(i, tm), 0, scale_blk)),
            pl.BlockSpec((None, 1, D_MODEL), lambda i: (_mod_row(i, tm), 0, shift_blk)),
        ],
        out_specs=pl.BlockSpec((tm, D_MODEL), lambda i: (i, 0)),
        out_shape=jax.ShapeDtypeStruct((rows, D_MODEL), BF16),
        compiler_params=_params(("parallel",)),
        name="norm_mod",
    )(*srcs, g.reshape(1, D_MODEL), mod, mod)


def _ws_kernel(*refs, has_rider, epilogue):
    if has_rider:
        x_ref, w_ref, rider_in, o_ref, rider_out, strip_ref = refs
        rider_out[...] = rider_in[...].astype(BF16)
    else:
        x_ref, w_ref, o_ref, strip_ref = refs

    @pl.when(pl.program_id(1) == 0)
    def _():
        strip_ref[...] = w_ref[...].astype(BF16)

    acc = jnp.dot(x_ref[...], strip_ref[...], preferred_element_type=F32)
    o_ref[...] = epilogue(acc).astype(o_ref.dtype)


def _matmul_ws(x, w, layer, col_blk_off, epilogue, *, rows, tm, tn, n_cols, row_blk_off=0, rider=None, name):
    kdim = x.shape[1]
    n_j, n_i = n_cols // tn, rows // tm
    in_specs = [
        pl.BlockSpec((tm, kdim), lambda j, i: (i + row_blk_off, 0)),
        pl.BlockSpec((None, kdim, tn), lambda j, i: (layer, 0, j + col_blk_off)),
    ]
    args = [x, w]
    out_specs = [pl.BlockSpec((tm, tn), lambda j, i: (i, j))]
    out_shape = [jax.ShapeDtypeStruct((rows, n_cols), BF16)]
    if rider is not None:
        rider_w, rider_layer, per_j = rider
        _, rider_rows, rider_cols = rider_w.shape
        rb = rider_rows // (n_j * per_j)

        def rider_blk(j, i):
            return j * per_j + jnp.minimum(i, per_j - 1)

        in_specs.append(pl.BlockSpec((None, rb, rider_cols), lambda j, i: (rider_layer, rider_blk(j, i), 0)))
        args.append(rider_w)
        out_specs.append(pl.BlockSpec((rb, rider_cols), lambda j, i: (rider_blk(j, i), 0)))
        out_shape.append(jax.ShapeDtypeStruct((rider_rows, rider_cols), BF16))
    return pl.pallas_call(
        functools.partial(_ws_kernel, has_rider=rider is not None, epilogue=epilogue),
        grid=(n_j, n_i),
        in_specs=in_specs,
        out_specs=out_specs,
        out_shape=out_shape,
        scratch_shapes=[pltpu.VMEM((kdim, tn), BF16)],
        compiler_params=_params(("arbitrary", "arbitrary")),
        name=name,
    )(*args)


def _mlp_down_kernel(*refs, nk, final, row_chunks):
    if final:
        x_ref, w_ref, h_ref, gate_ref, g_ref, out_ref, acc_ref = refs
        sc_ref = sh_ref = h_out = None
    else:
        x_ref, w_ref, h_ref, gate_ref, g_ref, sc_ref, sh_ref, h_out, out_ref, acc_ref = refs
    k = pl.program_id(1)
    j = pl.program_id(2)
    _, tm, tn = acc_ref.shape
    rc = tm // row_chunks

    @pl.when(k == 0)
    def _():
        acc_ref[j] = jnp.dot(x_ref[...], w_ref[...], preferred_element_type=F32)

    @pl.when(jnp.logical_and(k > 0, k < nk - 1))
    def _():
        acc_ref[j] += jnp.dot(x_ref[...], w_ref[...], preferred_element_type=F32)

    def h_new_rows(half, rows):
        acc = acc_ref[half, rows, :] + jnp.dot(x_ref[rows, :], w_ref[...], preferred_element_type=F32)
        h_new = h_ref[rows, :] + gate_ref[...] * acc
        if not final:
            h_out[rows, :] = h_new
        return h_new

    @pl.when(jnp.logical_and(k == nk - 1, j == 0))
    def _():
        for r in range(row_chunks):
            rows = pl.ds(r * rc, rc)
            acc_ref[0, rows, :] = h_new_rows(0, rows)

    @pl.when(jnp.logical_and(k == nk - 1, j == 1))
    def _():
        for r in range(row_chunks):
            rows = pl.ds(r * rc, rc)
            halves = (acc_ref[0, rows, :], h_new_rows(1, rows))
            ssq = sum(jnp.sum(hv * hv, axis=-1, keepdims=True) for hv in halves)
            rstd = lax.rsqrt(ssq * (1.0 / D_MODEL) + EPS)
            for half, hv in enumerate(halves):
                cols = slice(half * tn, (half + 1) * tn)
                y = hv * rstd * g_ref[:, cols]
                if not final:
                    y = y * (1.0 + sc_ref[:, cols]) + sh_ref[:, cols]
                out_ref[rows, cols] = y.astype(out_ref.dtype)


def _mlp_down(x, w, h, mod, g, mod_next, *, rows):
    tm, tk = MLP_DOWN_TILE
    tn = D_MODEL // 2
    nk = D_FF // tk
    final = mod_next is None

    def half_of(k, j):
        return jnp.where(k == nk - 1, j, 0)

    def mod_spec(blk, width):
        return pl.BlockSpec((None, 1, width), lambda i, k, j: (_mod_row(i, tm), 0, blk))

    in_specs = [
        pl.BlockSpec((tm, tk), lambda i, k, j: (i, k)),
        pl.BlockSpec((tk, tn), lambda i, k, j: (k, j)),
        pl.BlockSpec((tm, tn), lambda i, k, j: (i, half_of(k, j))),
        pl.BlockSpec((None, 1, tn), lambda i, k, j: (_mod_row(i, tm), 0, 5 * 2 + half_of(k, j))),
        pl.BlockSpec((1, D_MODEL), lambda i, k, j: (0, 0)),
    ]
    args = [x, w, h, mod, g.reshape(1, D_MODEL)]
    full_rows = pl.BlockSpec((tm, D_MODEL), lambda i, k, j: (i, 0))
    if final:
        out_specs = [full_rows]
        out_shape = [jax.ShapeDtypeStruct((rows, D_MODEL), F32)]
    else:
        in_specs += [mod_spec(1, D_MODEL), mod_spec(0, D_MODEL)]
        args += [mod_next, mod_next]
        out_specs = [pl.BlockSpec((tm, tn), lambda i, k, j: (i, half_of(k, j))), full_rows]
        out_shape = [jax.ShapeDtypeStruct((rows, D_MODEL), F32), jax.ShapeDtypeStruct((rows, D_MODEL), BF16)]
    return pl.pallas_call(
        functools.partial(_mlp_down_kernel, nk=nk, final=final, row_chunks=MLP_DOWN_ROW_CHUNKS),
        grid=(rows // tm, nk, 2),
        in_specs=in_specs,
        out_specs=out_specs,
        out_shape=out_shape,
        scratch_shapes=[pltpu.VMEM((2, tm, tn), F32)],
        compiler_params=_params(("arbitrary", "arbitrary", "arbitrary")),
        name="mlp_down",
    )(*args)


def _epi_relu2(acc):
    a = jnp.maximum(acc, 0.0)
    return a * a


def _rope(a, cos, sin_lo, sin_hi):
    up = pltpu.roll(a, HEAD_DIM - HEAD_DIM // 4, 1)
    dn = pltpu.roll(a, HEAD_DIM // 4, 1)
    return a * cos + up * sin_lo + dn * sin_hi


def _qkv_kernel(x_ref, w_ref, cq_ref, lq_ref, hq_ref, ck_ref, lk_ref, hk_ref, o_ref, kvc_ref, wb_ref,
                *, n_cast, n_lat_tiles, row_chunks):
    step = pl.program_id(0)
    kc = w_ref.shape[0]
    rc = x_ref.shape[0] // row_chunks
    kv_cols = slice(N_HEADS * HEAD_DIM, QKV_W)

    @pl.when(step < n_cast)
    def _():
        wb_ref[pl.ds(pl.multiple_of(step * kc, kc), kc), :] = w_ref[...].astype(BF16)

    @pl.when(step >= n_cast + n_lat_tiles)
    def _():
        for r in range(row_chunks):
            rows = pl.ds(r * rc, rc)
            acc = jnp.dot(x_ref[rows, :], wb_ref[:, kv_cols], preferred_element_type=F32)
            kvc_ref[rows, :] = acc.astype(kvc_ref.dtype)

    @pl.when(jnp.logical_and(step >= n_cast, step < n_cast + n_lat_tiles))
    def _():
        for r in range(row_chunks):
            rows = pl.ds(r * rc, rc)
            acc = jnp.dot(x_ref[rows, :], wb_ref[...], preferred_element_type=F32)
            q_tabs = (cq_ref[rows, :], lq_ref[rows, :], hq_ref[rows, :])
            k_tabs = (ck_ref[rows, :], lk_ref[rows, :], hk_ref[rows, :])
            for hh in range(N_HEADS + N_KV_HEADS):
                cols = slice(hh * HEAD_DIM, (hh + 1) * HEAD_DIM)
                tabs = q_tabs if hh < N_HEADS else k_tabs
                o_ref[rows, cols] = _rope(acc[:, cols], *tabs).astype(o_ref.dtype)
            v_cols = slice((N_HEADS + N_KV_HEADS) * HEAD_DIM, QKV_W)
            o_ref[rows, v_cols] = acc[:, v_cols].astype(o_ref.dtype)


def _qkv_latent(u, w_qkv, tables):
    tm = QKV_TM
    n_cast = CAST_STEPS
    kc = D_MODEL // n_cast
    pos_blocks = SEQ // tm
    n_lat, n_ctx = T_LAT // tm, T_CTX // tm
    kv_cols = 2 * N_KV_HEADS * HEAD_DIM

    def lat_step(s):
        return jnp.clip(s - n_cast, 0, n_lat - 1)

    def lat_tile(s):
        t = lat_step(s)
        return (t % BATCH) * pos_blocks + t // BATCH

    def x_tile(s):
        return jnp.where(s >= n_cast + n_lat, s - n_cast, lat_tile(s))

    def table_spec(group):
        return pl.BlockSpec((None, tm, HEAD_DIM), lambda s: (group, lat_step(s) // BATCH, 0))

    cos2, lo2, hi2 = tables
    return pl.pallas_call(
        functools.partial(_qkv_kernel, n_cast=n_cast, n_lat_tiles=n_lat, row_chunks=ROW_CHUNKS),
        grid=(n_cast + n_lat + n_ctx,),
        in_specs=[
            pl.BlockSpec((tm, D_MODEL), lambda s: (x_tile(s), 0)),
            pl.BlockSpec((None, kc, QKV_W), lambda s: (0, jnp.minimum(s, n_cast - 1), 0)),
            table_spec(0), table_spec(0), table_spec(0),
            table_spec(1), table_spec(1), table_spec(1),
        ],
        out_specs=[pl.BlockSpec((tm, QKV_W), lambda s: (lat_tile(s), 0)),
                   pl.BlockSpec((tm, kv_cols), lambda s: (jnp.maximum(s - n_cast - n_lat, 0), 0))],
        out_shape=[jax.ShapeDtypeStruct((T_LAT, QKV_W), BF16),
                   jax.ShapeDtypeStruct((T_CTX, kv_cols), BF16)],
        scratch_shapes=[pltpu.VMEM((D_MODEL, QKV_W), BF16)],
        compiler_params=_params(("arbitrary",)),
        name="qkv_proj",
    )(u, w_qkv, cos2, lo2, hi2, cos2, lo2, hi2)


def _proj_norm_kernel(*refs, n_src, n_cast, n_lat_tiles, row_chunks):
    x_ref, w_ref = refs[:2]
    h_refs = refs[2:2 + n_src]
    gate_ref, g_ref, sc_ref, sh_ref, h_out, u_out, wb_ref = refs[2 + n_src:]
    step = pl.program_id(0)
    kc = w_ref.shape[0]
    rc = x_ref.shape[0] // row_chunks

    @pl.when(step < n_cast)
    def _():
        wb_ref[pl.ds(pl.multiple_of(step * kc, kc), kc), :] = w_ref[...].astype(BF16)

    def compute(h_ref):
        for r in range(row_chunks):
            rows = pl.ds(r * rc, rc)
            acc = jnp.dot(x_ref[rows, :], wb_ref[...], preferred_element_type=F32)
            h_new = h_ref[rows, :] + gate_ref[...] * acc
            h_out[rows, :] = h_new
            u_out[rows, :] = _rms_mod(h_new, g_ref, sc_ref, sh_ref).astype(u_out.dtype)

    @pl.when(step >= n_cast)
    def _():
        _pick_stream(h_refs, step - n_cast, n_lat_tiles, compute)


def _proj_norm(x, w, layer, h_srcs, mod, g, rows):
    tm = PROJ_TM
    n_cast = CAST_STEPS
    kc = D_MODEL // n_cast

    def tile(s):
        return jnp.maximum(s - n_cast, 0)

    def mod_spec(blk):
        return pl.BlockSpec((None, 1, D_MODEL), lambda s: (_mod_row(tile(s), tm), 0, blk))

    body = functools.partial(_proj_norm_kernel, n_src=len(h_srcs), n_cast=n_cast,
                             n_lat_tiles=T_LAT // tm, row_chunks=ROW_CHUNKS)
    return pl.pallas_call(
        body,
        grid=(n_cast + rows // tm,),
        in_specs=[
            pl.BlockSpec((tm, D_MODEL), lambda s: (tile(s), 0)),
            pl.BlockSpec((None, kc, D_MODEL), lambda s: (layer, jnp.minimum(s, n_cast - 1), 0)),
        ] + _stream_specs(len(h_srcs), tm, tile) + [
            mod_spec(2),
            pl.BlockSpec((1, D_MODEL), lambda s: (0, 0)),
            mod_spec(4),
            mod_spec(3),
        ],
        out_specs=[pl.BlockSpec((tm, D_MODEL), lambda s: (tile(s), 0)),
                   pl.BlockSpec((tm, D_MODEL), lambda s: (tile(s), 0))],
        out_shape=[jax.ShapeDtypeStruct((rows, D_MODEL), F32),
                   jax.ShapeDtypeStruct((rows, D_MODEL), BF16)],
        scratch_shapes=[pltpu.VMEM((D_MODEL, D_MODEL), BF16)],
        compiler_params=_params(("arbitrary",)),
        name="proj_norm",
    )(x, w, *h_srcs, mod, g.reshape(1, D_MODEL), mod, mod)


def _conv_mixer_kernel(x_ref, wb_ref, wc_ref, wh_ref, cw_ref, o_ref, sb_ref, sc_ref, sh_ref, zk_ref, bk_ref,
                       *, row_chunks):
    i = pl.program_id(1)
    tm, tn = o_ref.shape
    rc = tm // row_chunks
    sub = F32_SUBLANES

    @pl.when(i == 0)
    def _():
        sb_ref[...] = wb_ref[...].astype(BF16)
        sc_ref[...] = wc_ref[...].astype(BF16)
        sh_ref[...] = wh_ref[...].astype(BF16)
        zk_ref[pl.ds(0, sub), :] = jnp.zeros((sub, tn), F32)
        zk_ref[pl.ds(sub + tm, sub), :] = jnp.zeros((sub, tn), F32)

    row0 = i * tm
    seq_len = jnp.where(row0 >= T_LAT, CTX_LEN, SEQ)

    def project(r):
        rows = pl.ds(r * rc, rc)
        x = x_ref[rows, :]
        bk_ref[rows, :] = jnp.dot(x, sb_ref[...], preferred_element_type=F32)
        zk_ref[pl.ds(sub + r * rc, rc), :] = (jnp.dot(x, sc_ref[...], preferred_element_type=F32)
                                               * jnp.dot(x, sh_ref[...], preferred_element_type=F32))

    def finish(r):
        base = r * rc
        pos = (row0 + base + lax.broadcasted_iota(jnp.int32, (rc, 1), 0)) & (seq_len - 1)
        zp = zk_ref[pl.ds(base, rc + 2 * sub), :]
        z_prev = jnp.where(pos == 0, 0.0, pltpu.roll(zp, 1, 0)[sub:sub + rc, :])
        z_next = jnp.where(pos == seq_len - 1, 0.0, pltpu.roll(zp, rc + 2 * sub - 1, 0)[sub:sub + rc, :])
        conv = (z_prev * cw_ref[pl.ds(0, 1), :] + zp[sub:sub + rc, :] * cw_ref[pl.ds(1, 1), :]
                + z_next * cw_ref[pl.ds(2, 1), :])
        o_ref[pl.ds(base, rc), :] = (bk_ref[pl.ds(base, rc), :] * conv).astype(o_ref.dtype)

    project(0)
    for r in range(1, row_chunks):
        project(r)
        finish(r - 1)
    finish(row_chunks - 1)


def _conv_mixer_in(u, w_in, conv_w, rows):
    tm, tn = CONV_TILE
    assert tm % CTX_LEN == 0 and T_LAT % tm == 0 and rows % tm == 0
    nblk = D_MODEL // tn

    def w_spec(part):
        return pl.BlockSpec((None, D_MODEL, tn), lambda j, i: (0, 0, j + part * nblk),
                            pipeline_mode=pl.Buffered(1))

    return pl.pallas_call(
        functools.partial(_conv_mixer_kernel, row_chunks=CONV_ROW_CHUNKS),
        grid=(nblk, rows // tm),
        in_specs=[
            pl.BlockSpec((tm, D_MODEL), lambda j, i: (i, 0)),
            w_spec(0), w_spec(1), w_spec(2),
            pl.BlockSpec((None, 3, tn), lambda j, i: (0, 0, j)),
        ],
        out_specs=pl.BlockSpec((tm, tn), lambda j, i: (i, j)),
        out_shape=jax.ShapeDtypeStruct((rows, D_MODEL), BF16),
        scratch_shapes=[pltpu.VMEM((D_MODEL, tn), BF16) for _ in range(3)] + [
            pltpu.VMEM((tm + 2 * F32_SUBLANES, tn), F32),
            pltpu.VMEM((tm, tn), F32),
        ],
        compiler_params=_params(("arbitrary", "arbitrary")),
        name="conv_mixer_in",
    )(u, w_in, w_in, w_in, conv_w)


def _attn_kernel(sink_ref, q_ref, k_ref, v_ref, kc_ref, vc_ref, o_ref, va_ref, vca_ref, kt_ref):
    kvh = pl.program_id(1)
    va_ref[:, :HEAD_DIM] = v_ref[...]
    va_ref[:, HEAD_DIM:] = jnp.ones((SEQ, HEAD_DIM), BF16)
    vca_ref[:, :HEAD_DIM] = vc_ref[...]
    vca_ref[:, HEAD_DIM:] = jnp.ones((CTX_LEN, HEAD_DIM), BF16)
    kt_ref[...] = k_ref[...].T
    kct = kc_ref[...].T
    vca = vca_ref[...]
    rel = (lax.broadcasted_iota(jnp.int32, (WINDOW, KWIN), 0)
           - lax.broadcasted_iota(jnp.int32, (WINDOW, KWIN), 1))
    masks = {}

    for n in range(SEQ // WINDOW):
        q0 = n * WINDOW
        start = min(max(q0 - WINDOW, 0), SEQ - KWIN)
        if q0 - start not in masks:
            masks[q0 - start] = jnp.abs(rel + (q0 - start)) <= WINDOW
        valid = masks[q0 - start]
        vwa = va_ref[pl.ds(start, KWIN), :]
        p_loc, p_ctx, sink_terms = [], [], []
        for g in range(GROUP):
            qg = q_ref[pl.ds(q0, WINDOW), g * HEAD_DIM:(g + 1) * HEAD_DIM]
            sl = jnp.where(valid, jnp.dot(qg, kt_ref[:, start:start + KWIN], preferred_element_type=F32), NEG)
            sc = jnp.dot(qg, kct, preferred_element_type=F32)
            sk = sink_ref[kvh * GROUP + g] * LOG2E
            m = jnp.maximum(jnp.max(jnp.concatenate([sl, sc], axis=1), axis=-1, keepdims=True), sk)
            p_loc.append(jnp.exp2(sl - m).astype(BF16))
            p_ctx.append(jnp.exp2(sc - m).astype(BF16))
            sink_terms.append(jnp.exp2(sk - m))
        oa = (jnp.dot(jnp.concatenate(p_loc, axis=0), vwa, preferred_element_type=F32)
              + jnp.dot(jnp.concatenate(p_ctx, axis=0), vca, preferred_element_type=F32))
        o = oa[:, :HEAD_DIM] / (oa[:, HEAD_DIM:] + jnp.concatenate(sink_terms, axis=0))
        for g in range(GROUP):
            o_ref[pl.ds(q0, WINDOW), g * HEAD_DIM:(g + 1) * HEAD_DIM] = (
                o[g * WINDOW:(g + 1) * WINDOW].astype(o_ref.dtype))


def _attention(qkv, kvc, sink):
    qw = GROUP * HEAD_DIM
    k_off = N_HEADS
    v_off = N_HEADS + N_KV_HEADS
    return pl.pallas_call(
        _attn_kernel,
        grid=(BATCH, N_KV_HEADS),
        in_specs=[
            pl.BlockSpec(memory_space=pltpu.SMEM),
            pl.BlockSpec((SEQ, qw), lambda b, h: (b, h)),
            pl.BlockSpec((SEQ, HEAD_DIM), lambda b, h: (b, k_off + h)),
            pl.BlockSpec((SEQ, HEAD_DIM), lambda b, h: (b, v_off + h)),
            pl.BlockSpec((CTX_LEN, HEAD_DIM), lambda b, h: (b, h)),
            pl.BlockSpec((CTX_LEN, HEAD_DIM), lambda b, h: (b, N_KV_HEADS + h)),
        ],
        out_specs=pl.BlockSpec((SEQ, qw), lambda b, h: (b, h)),
        out_shape=jax.ShapeDtypeStruct((T_LAT, N_HEADS * HEAD_DIM), BF16),
        scratch_shapes=[pltpu.VMEM((SEQ, 2 * HEAD_DIM), BF16), pltpu.VMEM((CTX_LEN, 2 * HEAD_DIM), BF16),
                        pltpu.VMEM((HEAD_DIM, SEQ), BF16)],
        compiler_params=_params(("parallel", "parallel")),
        name="window_attention",
    )(sink, qkv, qkv, qkv, kvc, kvc)


def _rope_tables():
    rows_n = SEQ // GRID_W
    row = jnp.repeat(jnp.arange(rows_n), GRID_W).astype(F32)
    col = jnp.tile(jnp.arange(GRID_W), rows_n).astype(F32)
    nf = HEAD_DIM // 4
    inv_freq = ROPE_BASE ** (-jnp.arange(nf, dtype=F32) / nf)
    ang_r = row[:, None] * inv_freq[None, :]
    ang_c = col[:, None] * inv_freq[None, :]
    ang = jnp.concatenate([ang_r, ang_r, ang_c, ang_c], axis=-1)
    cos, sin = jnp.cos(ang), jnp.sin(ang)
    low_half = (jnp.arange(HEAD_DIM) % (2 * nf)) < nf
    sin_lo = jnp.where(low_half[None, :], -sin, 0.0)
    sin_hi = jnp.where(low_half[None, :], 0.0, sin)
    scale = LOG2E / math.sqrt(HEAD_DIM)
    return (jnp.stack([cos * scale, cos]), jnp.stack([sin_lo * scale, sin_lo]),
            jnp.stack([sin_hi * scale, sin_hi]))


def _mlp(h, u2, mod, w1, w2, layer, g_next, mod_next, rows):
    tm, tn = MLP_UP_TILE
    a, w2_bf16 = _matmul_ws(u2, w1, layer, 0, _epi_relu2, rows=rows, tm=tm, tn=tn, n_cols=D_FF,
                            rider=(w2, layer, T_LAT // tm), name="mlp_up")
    return _mlp_down(a, w2_bf16, h, mod, g_next, mod_next, rows=rows)


def kernel(x, c, ctx, c_ctx, norm1_g, norm2_g, mod_w, mod_b, conv_w_in, conv_w, conv_w_out,
           attn_w_qkv, attn_sink, attn_w_o, mlp_w1, mlp_w2, final_g):
    x_lat = x.reshape(T_LAT, D_MODEL)
    x_ctx = ctx.reshape(T_CTX, D_MODEL)
    cvec = jnp.concatenate(
        [c, c_ctx[None, :], jnp.zeros((MOD_ROWS - BATCH - 1, D_MODEL), F32)], axis=0)
    mod = _mod_table(cvec, mod_w, mod_b)

    m0 = mod[0]
    u = _norm_mod([x_lat, x_ctx], norm1_g[0], m0, 0, 1, T_ALL)
    v = _conv_mixer_in(u, conv_w_in, conv_w, T_ALL)
    h, u2 = _proj_norm(v, conv_w_out, 0, [x_lat, x_ctx], m0, norm2_g[0], T_ALL)
    m1 = mod[1]
    h, u = _mlp(h, u2, m0, mlp_w1, mlp_w2, 0, norm1_g[1], m1, T_ALL)

    qkv, kvc = _qkv_latent(u, attn_w_qkv, _rope_tables())
    o = _attention(qkv, kvc, attn_sink[0])
    h, u2 = _proj_norm(o, attn_w_o, 0, [h], m1, norm2_g[1], T_LAT)
    out = _mlp(h, u2, m1, mlp_w1, mlp_w2, 1, final_g, None, T_LAT)[0]
    return out.reshape(BATCH, SEQ, D_MODEL)
```
